```python
import jax, jax.numpy as jnp
from jax import lax
import numpy as np

D_MODEL = 2048
BATCH = 4
SEQ = 4096
DEPTH = 4

CHUNK = 64
MIX_WIDTH = D_MODEL
POOL_WINDOWS = (2, 4, 8, 16)
POOL_GROUPS = len(POOL_WINDOWS)
POOL_WIDTH = MIX_WIDTH // 4
POOL_GROUP = POOL_WIDTH // POOL_GROUPS
CONV_WIDTH = MIX_WIDTH // 4
CONV_KERNEL = 31
ATTN_WIDTH = MIX_WIDTH - POOL_WIDTH - CONV_WIDTH
ATTN_HEADS = 8
ATTN_HEAD_DIM = ATTN_WIDTH // ATTN_HEADS
LEFT_CHUNKS = 8
BAND = (LEFT_CHUNKS + 1) * CHUNK
REL_LEFT = 128
REL_RIGHT = CHUNK - 1
REL_SIZE = REL_LEFT + REL_RIGHT + 1
OFF_CONV = POOL_WIDTH
OFF_GATE = OFF_CONV + CONV_WIDTH
OFF_Q = OFF_GATE + CONV_WIDTH
OFF_K = OFF_Q + ATTN_WIDTH
OFF_V = OFF_K + ATTN_WIDTH
IN_WIDTH = OFF_V + ATTN_WIDTH
PEER_HEADS = 8
PEER_NKEYS = 128
PEER_EXPERTS = PEER_NKEYS * PEER_NKEYS
PEER_QDIM = 256
PEER_HALF = PEER_QDIM // 2
PEER_TOPK = 16
PEER_TOKEN_BLOCK = 128
PLE_DIM = 256
LN_EPS = 1e-5
DEEPNORM_ALPHA = (2 * DEPTH) ** 0.25
DEEPNORM_BETA = (8 * DEPTH) ** -0.25

kernel_name = "hybrid_pool_conv_chunkattn_peer_deepnorm"


def layer_norm(x, g, b):
    xf = x.astype(jnp.float32)
    mu = jnp.mean(xf, axis=-1, keepdims=True)
    var = jnp.mean(jnp.square(xf - mu), axis=-1, keepdims=True)
    return ((xf - mu) * lax.rsqrt(var + LN_EPS) * g + b).astype(x.dtype)


def pool_mixer(xa, w_pool, pool_scale):
    b, s, _ = xa.shape
    xg = xa.reshape(b, s, POOL_GROUPS, POOL_GROUP).astype(jnp.float32)
    cs = jnp.concatenate([jnp.zeros_like(xg[:, :1]), jnp.cumsum(xg, axis=1)], axis=1)
    t = jnp.arange(s)
    pooled = []
    for g, w in enumerate(POOL_WINDOWS):
        lo = jnp.maximum(t + 1 - w, 0)
        win_sum = cs[:, 1:, g] - cs[:, lo, g]
        count = jnp.minimum(t + 1, w).astype(jnp.float32)
        pooled.append(win_sum / count[None, :, None])
    pooled = jnp.stack(pooled, axis=2) - xg
    y = jnp.einsum('bsgc,gcd->bsgd', pooled.astype(xa.dtype), w_pool)
    return y.reshape(b, s, POOL_WIDTH) * pool_scale


def conv_module(xv, xg, dw_w, dw_b, cn_g, cn_b):
    h = xv * jax.nn.sigmoid(xg)
    h = lax.conv_general_dilated(
        h, dw_w.astype(h.dtype), window_strides=(1,), padding=[(CONV_KERNEL - 1, 0)],
        dimension_numbers=('NWC', 'WIO', 'NWC'), feature_group_count=CONV_WIDTH) + dw_b
    return jax.nn.silu(layer_norm(h, cn_g, cn_b))


def chunk_attention(q, k, v, rel_table):
    b, s, _ = q.shape
    nc = s // CHUNK
    pad = LEFT_CHUNKS * CHUNK
    q = q.reshape(b, s, ATTN_HEADS, ATTN_HEAD_DIM)
    kp = jnp.pad(k.reshape(b, s, ATTN_HEADS, ATTN_HEAD_DIM), ((0, 0), (pad, 0), (0, 0), (0, 0)))
    vp = jnp.pad(v.reshape(b, s, ATTN_HEADS, ATTN_HEAD_DIM), ((0, 0), (pad, 0), (0, 0), (0, 0)))
    dist = jnp.arange(CHUNK)[:, None] - jnp.arange(BAND)[None, :] + pad
    rel_idx = jnp.clip(dist, -REL_RIGHT, REL_LEFT) + REL_RIGHT
    bias = rel_table[:, rel_idx].astype(jnp.float32)
    scale = ATTN_HEAD_DIM ** -0.5

    def one_chunk(c):
        start = c * CHUNK
        qc = lax.dynamic_slice_in_dim(q, start, CHUNK, axis=1)
        kb = lax.dynamic_slice_in_dim(kp, start, BAND, axis=1)
        vb = lax.dynamic_slice_in_dim(vp, start, BAND, axis=1)
        sc = jnp.einsum('bqhd,bkhd->bhqk', qc, kb).astype(jnp.float32) * scale + bias
        valid = (start - pad + jnp.arange(BAND)) >= 0
        sc = jnp.where(valid, sc, -jnp.inf)
        pr = jax.nn.softmax(sc, axis=-1).astype(vb.dtype)
        return jnp.einsum('bhqk,bkhd->bqhd', pr, vb)

    out = lax.map(one_chunk, jnp.arange(nc))
    return out.transpose(1, 0, 2, 3, 4).reshape(b, s, ATTN_WIDTH)


def peer(x, w_q, sub_keys, u_tab, v_tab):
    b, s, d = x.shape
    t = b * s
    xf = x.reshape(t, d)
    q = (xf @ w_q).reshape(t, PEER_HEADS, 2, PEER_HALF)
    sc = jnp.einsum('thpc,pkc->thpk', q, sub_keys).astype(jnp.float32)
    s1, i1 = lax.top_k(sc[:, :, 0], PEER_TOPK)
    s2, i2 = lax.top_k(sc[:, :, 1], PEER_TOPK)
    n_cand = PEER_TOPK * PEER_TOPK
    cand_s = (s1[..., :, None] + s2[..., None, :]).reshape(t, PEER_HEADS, n_cand)
    cand_i = (i1[..., :, None] * PEER_NKEYS + i2[..., None, :]).reshape(t, PEER_HEADS, n_cand)
    top_s, pos = lax.top_k(cand_s, PEER_TOPK)
    idx = jnp.take_along_axis(cand_i, pos, axis=-1)
    gate = jax.nn.softmax(top_s, axis=-1)
    nb = t // PEER_TOKEN_BLOCK
    sel = PEER_HEADS * PEER_TOPK
    xb = xf.reshape(nb, PEER_TOKEN_BLOCK, d)
    ib = idx.reshape(nb, PEER_TOKEN_BLOCK, sel)
    gb = gate.reshape(nb, PEER_TOKEN_BLOCK, sel).astype(x.dtype)

    def block(args):
        xt, it, gt = args
        u = u_tab[it]
        a = jax.nn.gelu(jnp.einsum('td,tkd->tk', xt, u), approximate=False) * gt
        return jnp.einsum('tk,tkd->td', a, v_tab[it])

    y = lax.map(block, (xb, ib, gb))
    return y.reshape(b, s, d)


def setup_inputs(seed: int = 0) -> dict:
    key = jax.random.key(seed)
    ks = jax.random.split(key, 24)

    def nrm(k, shape, scale):
        return jax.random.normal(k, shape, jnp.float32) * scale

    return {
        "x": nrm(ks[0], (BATCH, SEQ, D_MODEL), 1.0),
        "p": nrm(ks[1], (DEPTH, BATCH, SEQ, PLE_DIM), 1.0),
        "w_in": nrm(ks[2], (DEPTH, D_MODEL, IN_WIDTH), D_MODEL ** -0.5),
        "b_in": nrm(ks[3], (DEPTH, IN_WIDTH), 0.02),
        "w_pool": nrm(ks[4], (DEPTH, POOL_GROUPS, POOL_GROUP, POOL_GROUP), POOL_GROUP ** -0.5),
        "pool_scale": 1.0 + nrm(ks[5], (DEPTH, POOL_WIDTH), 0.1),
        "dw_w": nrm(ks[6], (DEPTH, CONV_KERNEL, 1, CONV_WIDTH), CONV_KERNEL ** -0.5),
        "dw_b": nrm(ks[7], (DEPTH, CONV_WIDTH), 0.02),
        "cn_g": 1.0 + nrm(ks[8], (DEPTH, CONV_WIDTH), 0.05),
        "cn_b": nrm(ks[9], (DEPTH, CONV_WIDTH), 0.02),
        "rel_bias": nrm(ks[10], (DEPTH, ATTN_HEADS, REL_SIZE), 0.5),
        "w_out": nrm(ks[11], (DEPTH, MIX_WIDTH, D_MODEL), MIX_WIDTH ** -0.5 * DEEPNORM_BETA),
        "ln1_g": 1.0 + nrm(ks[12], (DEPTH, D_MODEL), 0.05),
        "ln1_b": nrm(ks[13], (DEPTH, D_MODEL), 0.02),
        "w_q": nrm(ks[14], (DEPTH, D_MODEL, PEER_HEADS * PEER_QDIM), D_MODEL ** -0.5),
        "sub_keys": nrm(ks[15], (DEPTH, 2, PEER_NKEYS, PEER_HALF), PEER_HALF ** -0.5),
        "u_tab": nrm(ks[16], (DEPTH, PEER_EXPERTS, D_MODEL), D_MODEL ** -0.5),
        "v_tab": nrm(ks[17], (DEPTH, PEER_EXPERTS, D_MODEL), DEEPNORM_BETA * PEER_HEADS ** -0.5),
        "w_ple": nrm(ks[18], (DEPTH, PLE_DIM, D_MODEL), PLE_DIM ** -0.5 * DEEPNORM_BETA),
        "w_pg": nrm(ks[19], (DEPTH, D_MODEL, D_MODEL), D_MODEL ** -0.5),
        "ln2_g": 1.0 + nrm(ks[20], (DEPTH, D_MODEL), 0.05),
        "ln2_b": nrm(ks[21], (DEPTH, D_MODEL), 0.02),
    }


def reference(x, p, w_in, b_in, w_pool, pool_scale, dw_w, dw_b, cn_g, cn_b, rel_bias, w_out,
              ln1_g, ln1_b, w_q, sub_keys, u_tab, v_tab, w_ple, w_pg, ln2_g, ln2_b):
    for i in range(DEPTH):
        h = x @ w_in[i] + b_in[i]
        y_pool = pool_mixer(h[..., :OFF_CONV], w_pool[i], pool_scale[i])
        y_conv = conv_module(h[..., OFF_CONV:OFF_GATE], h[..., OFF_GATE:OFF_Q],
                             dw_w[i], dw_b[i], cn_g[i], cn_b[i])
        y_attn = chunk_attention(h[..., OFF_Q:OFF_K], h[..., OFF_K:OFF_V], h[..., OFF_V:], rel_bias[i])
        mixed = jnp.concatenate([y_pool, y_conv, y_attn], axis=-1) @ w_out[i]
        x = layer_norm(DEEPNORM_ALPHA * x + mixed, ln1_g[i], ln1_b[i])
        ffn = peer(x, w_q[i], sub_keys[i], u_tab[i], v_tab[i])
        ple = (p[i] @ w_ple[i]) * jax.nn.sigmoid(x @ w_pg[i])
        x = layer_norm(DEEPNORM_ALPHA * x + ffn + ple, ln2_g[i], ln2_b[i])
    return x
```

```python
import functools

import jax
import jax.numpy as jnp
from jax import lax
from jax.experimental import pallas as pl
from jax.experimental.pallas import tpu as pltpu

F32 = jnp.float32
BF16 = jnp.bfloat16

D_MODEL = 2048
CHUNK = 64
POOL_WINDOWS = (2, 4, 8, 16)
POOL_GROUP = 128
POOL_WIDTH = 512
CONV_WIDTH = 512
CONV_KERNEL = 31
ATTN_WIDTH = 1024
ATTN_HEADS = 8
HEAD_DIM = 128
LEFT_CHUNKS = 8
BAND = (LEFT_CHUNKS + 1) * CHUNK
REL_LEFT = 128
REL_RIGHT = CHUNK - 1
PC_WIDTH = POOL_WIDTH + 2 * CONV_WIDTH
PEER_HEADS = 8
PEER_NKEYS = 128
PEER_EXPERTS = PEER_NKEYS * PEER_NKEYS
PEER_HALF = 128
PEER_TOPK = 16
LN_EPS = 1e-5
DEPTH = 4
DEEPNORM_ALPHA = (2 * DEPTH) ** 0.25

LANES = 128
SUBLANES = 8
VMEM_LIMIT = 56 * 1024 * 1024

HALO = 32
ATTN_TQ = 256
ATTN_KBLKS = 3
NEG_BIG = -1e30
SQRT_HALF = 0.7071067811865476


def _cparams(n_axes):
    return pltpu.CompilerParams(dimension_semantics=("arbitrary",) * n_axes,
                                vmem_limit_bytes=VMEM_LIMIT)


def _layer_norm(z, g, b):
    mu = jnp.mean(z, axis=-1, keepdims=True)
    zc = z - mu
    var = jnp.mean(zc * zc, axis=-1, keepdims=True)
    return zc * lax.rsqrt(var + LN_EPS) * g + b


def _mm_bias_kernel(x_ref, w_ref, b_ref, o_ref):
    acc = jnp.dot(x_ref[...], w_ref[...], preferred_element_type=F32)
    o_ref[...] = (acc + b_ref[...]).astype(o_ref.dtype)


def _matmul_bias(xb, w, b, out_dtype, tm, tn, name):
    t, k = xb.shape
    n = w.shape[1]
    return pl.pallas_call(
        _mm_bias_kernel,
        grid=(n // tn, t // tm),
        in_specs=[pl.BlockSpec((tm, k), lambda j, i: (i, 0)),
                  pl.BlockSpec((k, tn), lambda j, i: (0, j)),
                  pl.BlockSpec((1, tn), lambda j, i: (0, j))],
        out_specs=pl.BlockSpec((tm, tn), lambda j, i: (i, j)),
        out_shape=jax.ShapeDtypeStruct((t, n), out_dtype),
        compiler_params=_cparams(2),
        name=name,
    )(xb, w, b)


def _mixer_kernel(cur_ref, halo_ref, wpool_ref, pscale_ref, dww_ref, dwb_ref, cng_ref, cnb_ref,
                  o_ref, *, ts):
    s = pl.program_id(1)
    cur = cur_ref[...]
    halo = jnp.where(s > 0, halo_ref[...], 0.0)
    ext = jnp.concatenate([halo, cur], axis=0)

    pos = s * ts + lax.broadcasted_iota(jnp.int32, (ts, POOL_GROUP), 0)
    pool_out = []
    for g, w in enumerate(POOL_WINDOWS):
        xg = ext[:, g * POOL_GROUP:(g + 1) * POOL_GROUP]
        acc = xg
        step = 1
        while step < w:
            acc = acc + pltpu.roll(acc, step, axis=0)
            step *= 2
        count = jnp.minimum(pos + 1, w).astype(F32)
        pooled = acc[HALO:] / count - xg[HALO:]
        pool_out.append(jnp.dot(pooled.astype(BF16), wpool_ref[g], preferred_element_type=F32))
    y_pool = jnp.concatenate(pool_out, axis=1) * pscale_ref[...]

    xv = ext[:, POOL_WIDTH:POOL_WIDTH + CONV_WIDTH]
    xg = ext[:, POOL_WIDTH + CONV_WIDTH:]
    hglu = xv * jax.nn.sigmoid(xg)
    acc = jnp.zeros((ts, CONV_WIDTH), F32)
    for k in range(CONV_KERNEL):
        delay = CONV_KERNEL - 1 - k
        shifted = hglu if delay == 0 else pltpu.roll(hglu, delay, axis=0)
        acc = acc + dww_ref[k:k + 1, :] * shifted[HALO:]
    hc = _layer_norm(acc + dwb_ref[...], cng_ref[...], cnb_ref[...])
    y_conv = hc * jax.nn.sigmoid(hc)
    o_ref[...] = jnp.concatenate([y_pool, y_conv], axis=1).astype(o_ref.dtype)


def _mixer(hpc, wpool, pscale, dww, dwb, cng, cnb, batch, seq, ts):
    hpc3 = hpc.reshape(batch, seq, PC_WIDTH)
    hb = ts // HALO
    out = pl.pallas_call(
        functools.partial(_mixer_kernel, ts=ts),
        grid=(batch, seq // ts),
        in_specs=[pl.BlockSpec((None, ts, PC_WIDTH), lambda b, s: (b, s, 0)),
                  pl.BlockSpec((None, HALO, PC_WIDTH), lambda b, s: (b, jnp.maximum(s * hb - 1, 0), 0)),
                  pl.BlockSpec((4, POOL_GROUP, POOL_GROUP), lambda b, s: (0, 0, 0)),
                  pl.BlockSpec((1, POOL_WIDTH), lambda b, s: (0, 0)),
                  pl.BlockSpec((CONV_KERNEL, CONV_WIDTH), lambda b, s: (0, 0)),
                  pl.BlockSpec((1, CONV_WIDTH), lambda b, s: (0, 0)),
                  pl.BlockSpec((1, CONV_WIDTH), lambda b, s: (0, 0)),
                  pl.BlockSpec((1, CONV_WIDTH), lambda b, s: (0, 0))],
        out_specs=pl.BlockSpec((None, ts, POOL_WIDTH + CONV_WIDTH), lambda b, s: (b, s, 0)),
        out_shape=jax.ShapeDtypeStruct((batch, seq, POOL_WIDTH + CONV_WIDTH), BF16),
        compiler_params=_cparams(2),
        name="pool_conv_mixer",
    )(hpc3, hpc3, wpool, pscale, dww, dwb, cng, cnb)
    return out.reshape(batch * seq, POOL_WIDTH + CONV_WIDTH)


def _attn_kernel(q_ref, k0_ref, k1_ref, k2_ref, v0_ref, v1_ref, v2_ref, bias_ref, o_ref):
    i = pl.program_id(1)
    k_refs = (k0_ref, k1_ref, k2_ref)
    v_refs = (v0_ref, v1_ref, v2_ref)
    scale = HEAD_DIM ** -0.5
    for h in range(ATTN_HEADS):
        cols = slice(h * HEAD_DIM, (h + 1) * HEAD_DIM)
        q = q_ref[:, cols]
        scores = []
        for r in range(ATTN_KBLKS):
            sc = lax.dot_general(q, k_refs[r][:, cols], (((1,), (1,)), ((), ())),
                                 preferred_element_type=F32)
            sc = sc * scale + bias_ref[h, :, r * ATTN_TQ:(r + 1) * ATTN_TQ]
            scores.append(jnp.where(i - (ATTN_KBLKS - 1) + r >= 0, sc, NEG_BIG))
        m = scores[0].max(axis=1, keepdims=True)
        for r in range(1, ATTN_KBLKS):
            m = jnp.maximum(m, scores[r].max(axis=1, keepdims=True))
        denom = jnp.zeros_like(m)
        acc = jnp.zeros((ATTN_TQ, HEAD_DIM), F32)
        for r in range(ATTN_KBLKS):
            e = jnp.exp(scores[r] - m)
            denom = denom + e.sum(axis=1, keepdims=True)
            acc = acc + jnp.dot(e.astype(BF16), v_refs[r][:, cols], preferred_element_type=F32)
        o_ref[:, cols] = (acc / denom).astype(o_ref.dtype)


def _attention(qkv, bias_tile, batch, seq):
    qkv3 = qkv.reshape(batch, seq, 3 * ATTN_WIDTH)
    blk = (None, ATTN_TQ, ATTN_WIDTH)

    def kv_spec(col, r):
        return pl.BlockSpec(blk, lambda b, i: (b, jnp.maximum(i - (ATTN_KBLKS - 1) + r, 0), col))

    out = pl.pallas_call(
        _attn_kernel,
        grid=(batch, seq // ATTN_TQ),
        in_specs=[pl.BlockSpec(blk, lambda b, i: (b, i, 0))]
                 + [kv_spec(1, r) for r in range(ATTN_KBLKS)]
                 + [kv_spec(2, r) for r in range(ATTN_KBLKS)]
                 + [pl.BlockSpec((ATTN_HEADS, ATTN_TQ, ATTN_KBLKS * ATTN_TQ), lambda b, i: (0, 0, 0))],
        out_specs=pl.BlockSpec(blk, lambda b, i: (b, i, 0)),
        out_shape=jax.ShapeDtypeStruct((batch, seq, ATTN_WIDTH), BF16),
        compiler_params=_cparams(2),
        name="chunk_attention",
    )(qkv3, qkv3, qkv3, qkv3, qkv3, qkv3, qkv3, bias_tile)
    return out.reshape(batch * seq, ATTN_WIDTH)


def _attn_bias_tile(rel_table):
    qpos = jnp.arange(ATTN_TQ)[:, None] + (ATTN_KBLKS - 1) * ATTN_TQ
    kpos = jnp.arange(ATTN_KBLKS * ATTN_TQ)[None, :]
    dist = qpos - kpos
    rel_idx = jnp.clip(dist, -REL_RIGHT, REL_LEFT) + REL_RIGHT
    chunk_start = (qpos // CHUNK) * CHUNK
    in_band = (kpos >= chunk_start - LEFT_CHUNKS * CHUNK) & (kpos < chunk_start + CHUNK)
    bias = rel_table[:, rel_idx].astype(F32)
    return jnp.where(in_band[None], bias, NEG_BIG)


def _outproj_kernel(ypc_ref, ya_ref, w_ref, x_ref, g_ref, b_ref, x1_ref, x1t_ref):
    half = POOL_WIDTH + CONV_WIDTH
    mixed = jnp.dot(ypc_ref[...], w_ref[:half, :], preferred_element_type=F32)
    mixed = mixed + jnp.dot(ya_ref[...], w_ref[half:, :], preferred_element_type=F32)
    y = _layer_norm(DEEPNORM_ALPHA * x_ref[...] + mixed, g_ref[...], b_ref[...])
    x1_ref[...] = y
    x1t_ref[...] = y.T.astype(BF16)


def _outproj(ypc, ya, w_out_b, x, g, b, tm):
    t = x.shape[0]
    return pl.pallas_call(
        _outproj_kernel,
        grid=(t // tm,),
        in_specs=[pl.BlockSpec((tm, POOL_WIDTH + CONV_WIDTH), lambda i: (i, 0)),
                  pl.BlockSpec((tm, ATTN_WIDTH), lambda i: (i, 0)),
                  pl.BlockSpec((D_MODEL, D_MODEL), lambda i: (0, 0)),
                  pl.BlockSpec((tm, D_MODEL), lambda i: (i, 0)),
                  pl.BlockSpec((1, D_MODEL), lambda i: (0, 0)),
                  pl.BlockSpec((1, D_MODEL), lambda i: (0, 0))],
        out_specs=[pl.BlockSpec((tm, D_MODEL), lambda i: (i, 0)),
                   pl.BlockSpec((D_MODEL, tm), lambda i: (0, i))],
        out_shape=[jax.ShapeDtypeStruct((t, D_MODEL), F32),
                   jax.ShapeDtypeStruct((D_MODEL, t), BF16)],
        compiler_params=_cparams(1),
        name="outproj_ln1",
    )(ypc, ya, w_out_b, x, g, b)


def _extract_topk(cur, order, k):
    rank = jnp.full(cur.shape, float(k), F32)
    vals = []
    for r in range(k):
        m = jnp.max(cur, axis=0, keepdims=True)
        first = jnp.min(jnp.where(cur == m, order, 1e9), axis=0, keepdims=True)
        hit = order == first
        rank = jnp.where(hit, float(r), rank)
        cur = jnp.where(hit, -jnp.inf, cur)
        vals.append(m)
    return vals, rank


_MID_K2 = tuple(range(1, 8))


def _route_kernel(xt_ref, wqt_ref, keys_ref, n_ref, ea_ref, rb_ref, eb_ref, qt_scr, s_scr, *, tp):
    qt_scr[...] = jnp.dot(wqt_ref[...], xt_ref[...], preferred_element_type=F32)
    key_order = lax.broadcasted_iota(jnp.int32, (PEER_NKEYS, tp), 0).astype(F32)
    row8 = lax.broadcasted_iota(jnp.int32, (SUBLANES, tp), 0).astype(F32)
    row16 = lax.broadcasted_iota(jnp.int32, (2 * SUBLANES, tp), 0).astype(F32)

    def head(h, carry):
        base = pl.multiple_of(h * 2 * PEER_HALF, 2 * PEER_HALF)
        sc, ranks = [], []
        for p in range(2):
            qh = qt_scr[pl.ds(base + p * PEER_HALF, PEER_HALF), :].astype(BF16)
            s = jnp.dot(keys_ref[p], qh, preferred_element_type=F32)
            vals, rank = _extract_topk(s, key_order, PEER_TOPK)
            for r in range(PEER_TOPK):
                s_scr[p, r:r + 1, :] = vals[r]
            sc.append(s)
            ranks.append(rank)
        s1 = s_scr[0]
        s2 = s_scr[1]
        cands = [s1 + s2[0:1, :]]
        order = [row16 * PEER_TOPK]
        for k2 in _MID_K2:
            valid = row8 < float(PEER_TOPK // (k2 + 1))
            cands.append(jnp.where(valid, s1[:SUBLANES] + s2[k2:k2 + 1, :], -jnp.inf))
            order.append(row8 * PEER_TOPK + float(k2))
        cands.append(s1[0:1, :] + s2[SUBLANES:])
        order.append(row8 + float(SUBLANES))
        cand = jnp.concatenate(cands, axis=0)
        cvals, crank = _extract_topk(cand, jnp.concatenate(order, axis=0), PEER_TOPK)
        sel = crank < float(PEER_TOPK)
        self_ = sel.astype(F32)
        top = s1[0:1, :] + s2[0:1, :]
        z = jnp.sum(jnp.where(sel, jnp.exp(cand - top), 0.0), axis=0, keepdims=True)
        n_lo = self_[0:SUBLANES]
        for g in range(len(_MID_K2)):
            lo = 2 * SUBLANES + g * SUBLANES
            n_lo = n_lo + self_[lo:lo + SUBLANES]
        tail = jnp.sum(self_[2 * SUBLANES + len(_MID_K2) * SUBLANES:], axis=0, keepdims=True)
        n_lo = n_lo + jnp.where(row8 == 0.0, tail, 0.0)
        n16 = jnp.concatenate([n_lo, self_[SUBLANES:2 * SUBLANES]], axis=0)
        nkey = jnp.zeros((PEER_NKEYS, tp), F32)
        for k1 in range(PEER_TOPK):
            nkey = jnp.where(ranks[0] == float(k1), n16[k1:k1 + 1, :], nkey)
        n_ref[h] = nkey
        ea_ref[h] = jnp.exp(sc[0] - s1[0:1, :]) / z
        rb_ref[h] = ranks[1]
        eb_ref[h] = jnp.exp(sc[1] - s2[0:1, :])
        return carry

    lax.fori_loop(0, PEER_HEADS, head, 0)


def _route(x1t, wqt_b, keys_b, tp):
    t = x1t.shape[1]
    gate_spec = pl.BlockSpec((PEER_HEADS, PEER_NKEYS, tp), lambda i: (0, 0, i))
    gate_shape = jax.ShapeDtypeStruct((PEER_HEADS, PEER_NKEYS, t), F32)
    return pl.pallas_call(
        functools.partial(_route_kernel, tp=tp),
        grid=(t // tp,),
        in_specs=[pl.BlockSpec((D_MODEL, tp), lambda i: (0, i)),
                  pl.BlockSpec((D_MODEL, D_MODEL), lambda i: (0, 0)),
                  pl.BlockSpec((2, PEER_NKEYS, PEER_HALF), lambda i: (0, 0, 0))],
        out_specs=[gate_spec] * 4,
        out_shape=[gate_shape] * 4,
        scratch_shapes=[pltpu.VMEM((D_MODEL, tp), F32),
                        pltpu.VMEM((2, PEER_TOPK, tp), F32)],
        compiler_params=_cparams(1),
        name="peer_route",
    )(x1t, wqt_b, keys_b)


PEER_TE = 512
PEER_SUB = 256
PEER_ROWS = 32


def _experts_kernel(xt_ref, u_ref, vt_ref, n_ref, ea_ref, rb_ref, eb_ref, y_ref, h_scr, a_scr, *, tm):
    j = pl.program_id(1)

    @pl.when(j == 0)
    def _():
        y_ref[...] = jnp.zeros_like(y_ref)

    for sb in range(PEER_TE // PEER_SUB):
        e0 = sb * PEER_SUB
        h_scr[e0:e0 + PEER_SUB, :] = jnp.dot(u_ref[e0:e0 + PEER_SUB, :], xt_ref[...],
                                             preferred_element_type=F32)
        for ii in range(PEER_SUB // PEER_NKEYS):
            i = j * (PEER_TE // PEER_NKEYS) + sb * (PEER_SUB // PEER_NKEYS) + ii
            n_rows = [n_ref[h, pl.ds(i, 1), :] for h in range(PEER_HEADS)]
            ea_rows = [ea_ref[h, pl.ds(i, 1), :] for h in range(PEER_HEADS)]
            for c in range(PEER_NKEYS // PEER_ROWS):
                r0 = c * PEER_ROWS
                gate = None
                for h in range(PEER_HEADS):
                    picked = jnp.where(rb_ref[h, r0:r0 + PEER_ROWS, :] < n_rows[h],
                                       eb_ref[h, r0:r0 + PEER_ROWS, :], 0.0)
                    term = picked * ea_rows[h]
                    gate = term if gate is None else gate + term
                row = e0 + ii * PEER_NKEYS + r0
                hh = h_scr[row:row + PEER_ROWS, :]
                act = 0.5 * hh * (1.0 + lax.erf(hh * SQRT_HALF))
                a_scr[row:row + PEER_ROWS, :] = (act * gate).astype(BF16)
        y_ref[...] += jnp.dot(vt_ref[:, e0:e0 + PEER_SUB], a_scr[e0:e0 + PEER_SUB, :],
                              preferred_element_type=F32)


def _experts(x1t, u_b, vt_b, gates, tm):
    t = x1t.shape[1]
    gate_spec = pl.BlockSpec((PEER_HEADS, PEER_NKEYS, tm), lambda i, j: (0, 0, i))
    return pl.pallas_call(
        functools.partial(_experts_kernel, tm=tm),
        grid=(t // tm, PEER_EXPERTS // PEER_TE),
        in_specs=[pl.BlockSpec((D_MODEL, tm), lambda i, j: (0, i)),
                  pl.BlockSpec((PEER_TE, D_MODEL), lambda i, j: (j, 0)),
                  pl.BlockSpec((D_MODEL, PEER_TE), lambda i, j: (0, j))] + [gate_spec] * 4,
        out_specs=pl.BlockSpec((D_MODEL, tm), lambda i, j: (0, i)),
        out_shape=jax.ShapeDtypeStruct((D_MODEL, t), F32),
        scratch_shapes=[pltpu.VMEM((PEER_TE, tm), F32),
                        pltpu.VMEM((PEER_TE, tm), BF16)],
        compiler_params=_cparams(2),
        name="peer_experts",
    )(x1t, u_b, vt_b, *gates)


def _final_kernel(x1_ref, yt_ref, p_ref, wple_ref, wpg_ref, g_ref, b_ref, x2_ref, x2b_ref):
    x1 = x1_ref[...]
    emb = jnp.dot(p_ref[...].astype(BF16), wple_ref[...], preferred_element_type=F32)
    gate = jax.nn.sigmoid(jnp.dot(x1.astype(BF16), wpg_ref[...], preferred_element_type=F32))
    z = DEEPNORM_ALPHA * x1 + yt_ref[...].T + emb * gate
    y = _layer_norm(z, g_ref[...], b_ref[...])
    x2_ref[...] = y
    x2b_ref[...] = y.astype(BF16)


def _final(x1, yt, p, wple_b, wpg_b, g, b, tm):
    t = x1.shape[0]
    pdim = p.shape[1]
    return pl.pallas_call(
        _final_kernel,
        grid=(t // tm,),
        in_specs=[pl.BlockSpec((tm, D_MODEL), lambda i: (i, 0)),
                  pl.BlockSpec((D_MODEL, tm), lambda i: (0, i)),
                  pl.BlockSpec((tm, pdim), lambda i: (i, 0)),
                  pl.BlockSpec((pdim, D_MODEL), lambda i: (0, 0)),
                  pl.BlockSpec((D_MODEL, D_MODEL), lambda i: (0, 0)),
                  pl.BlockSpec((1, D_MODEL), lambda i: (0, 0)),
                  pl.BlockSpec((1, D_MODEL), lambda i: (0, 0))],
        out_specs=[pl.BlockSpec((tm, D_MODEL), lambda i: (i, 0)),
                   pl.BlockSpec((tm, D_MODEL), lambda i: (i, 0))],
        out_shape=[jax.ShapeDtypeStruct((t, D_MODEL), F32),
                   jax.ShapeDtypeStruct((t, D_MODEL), BF16)],
        compiler_params=_cparams(1),
        name="ple_ln2",
    )(x1, yt, p, wple_b, wpg_b, g, b)


def _tiles(batch, seq):
    t = batch * seq
    return dict(
        mm_tm=min(512, t), mm_tn=1536,
        mixer_ts=min(512, seq),
        out_tm=min(256, t),
        route_tp=min(256, t),
        peer_tm=min(512, t),
        final_tm=min(256, t),
    )


def kernel(x, p, w_in, b_in, w_pool, pool_scale, dw_w, dw_b, cn_g, cn_b, rel_bias, w_out, ln1_g, ln1_b,
           w_q, sub_keys, u_tab, v_tab, w_ple, w_pg, ln2_g, ln2_b):
    batch, seq, d = x.shape
    depth = w_in.shape[0]
    t = batch * seq
    assert d == D_MODEL and depth == DEPTH and seq % ATTN_TQ == 0
    tl = _tiles(batch, seq)

    xf = x.reshape(t, d)
    xb = xf.astype(BF16)
    for i in range(depth):
        w_in_b = w_in[i].astype(BF16)
        b_row = b_in[i].reshape(1, -1)
        hpc = _matmul_bias(xb, w_in_b[:, :PC_WIDTH], b_row[:, :PC_WIDTH], F32,
                           tl["mm_tm"], PC_WIDTH, "inproj_pool_conv")
        qkv = _matmul_bias(xb, w_in_b[:, PC_WIDTH:], b_row[:, PC_WIDTH:], BF16,
                           tl["mm_tm"], tl["mm_tn"], "inproj_qkv")
        ypc = _mixer(hpc, w_pool[i].astype(BF16), pool_scale[i].reshape(1, -1),
                     dw_w[i].reshape(CONV_KERNEL, CONV_WIDTH), dw_b[i].reshape(1, -1),
                     cn_g[i].reshape(1, -1), cn_b[i].reshape(1, -1), batch, seq, tl["mixer_ts"])
        ya = _attention(qkv, _attn_bias_tile(rel_bias[i]), batch, seq)
        x1, x1t = _outproj(ypc, ya, w_out[i].astype(BF16), xf, ln1_g[i].reshape(1, -1),
                           ln1_b[i].reshape(1, -1), tl["out_tm"])
        gates = _route(x1t, w_q[i].T.astype(BF16), sub_keys[i].astype(BF16), tl["route_tp"])
        yt = _experts(x1t, u_tab[i].astype(BF16), v_tab[i].T.astype(BF16), gates, tl["peer_tm"])
        xf, xb = _final(x1, yt, p[i].reshape(t, -1), w_ple[i].astype(BF16), w_pg[i].astype(BF16),
                        ln2_g[i].reshape(1, -1), ln2_b[i].reshape(1, -1), tl["final_tm"])
    return xf.reshape(batch, seq, d)
```

```python
import functools

import jax
import jax.numpy as jnp
import numpy as np
from jax import lax
from jax.experimental import pallas as pl
from jax.experimental.pallas import tpu as pltpu

F32 = jnp.float32
BF16 = jnp.bfloat16

D_MODEL = 2048
CHUNK = 64
POOL_WINDOWS = (2, 4, 8, 16)
POOL_GROUP = 128
POOL_WIDTH = 512
CONV_WIDTH = 512
CONV_KERNEL = 31
ATTN_WIDTH = 1024
ATTN_HEADS = 8
HEAD_DIM = 128
LEFT_CHUNKS = 8
BAND = (LEFT_CHUNKS + 1) * CHUNK
REL_LEFT = 128
REL_RIGHT = CHUNK - 1
PC_WIDTH = POOL_WIDTH + 2 * CONV_WIDTH
PEER_HEADS = 8
PEER_NKEYS = 128
PEER_EXPERTS = PEER_NKEYS * PEER_NKEYS
PEER_HALF = 128
PEER_TOPK = 16
LN_EPS = 1e-5
DEPTH = 4
DEEPNORM_ALPHA = (2 * DEPTH) ** 0.25

LANES = 128
SUBLANES = 8
VMEM_LIMIT = 56 * 1024 * 1024

HALO = 32
ATTN_TQ = 256
ATTN_KBLKS = 3
NEG_BIG = -1e30
SQRT_HALF = 0.7071067811865476


def _cparams(n_axes, flags=None):
    return pltpu.CompilerParams(dimension_semantics=("arbitrary",) * n_axes,
                                vmem_limit_bytes=VMEM_LIMIT, flags=flags)


def _layer_norm(z, g, b):
    mu = jnp.mean(z, axis=-1, keepdims=True)
    zc = z - mu
    var = jnp.mean(zc * zc, axis=-1, keepdims=True)
    return zc * lax.rsqrt(var + LN_EPS) * g + b


def _mm_bias_kernel(x_ref, w_ref, b_ref, o_ref):
    acc = jnp.dot(x_ref[...], w_ref[...], preferred_element_type=F32)
    o_ref[...] = (acc + b_ref[...]).astype(o_ref.dtype)


def _matmul_bias(xb, w, b, out_dtype, tm, tn, name):
    t, k = xb.shape
    n = w.shape[1]
    return pl.pallas_call(
        _mm_bias_kernel,
        grid=(n // tn, t // tm),
        in_specs=[pl.BlockSpec((tm, k), lambda j, i: (i, 0)),
                  pl.BlockSpec((k, tn), lambda j, i: (0, j)),
                  pl.BlockSpec((1, tn), lambda j, i: (0, j))],
        out_specs=pl.BlockSpec((tm, tn), lambda j, i: (i, j)),
        out_shape=jax.ShapeDtypeStruct((t, n), out_dtype),
        compiler_params=_cparams(2),
        name=name,
    )(xb, w, b)


def _mixer_kernel(cur_ref, halo_ref, wpool_ref, pscale_ref, dww_ref, dwb_ref, cng_ref, cnb_ref,
                  o_ref, *, ts):
    s = pl.program_id(1)
    cur = cur_ref[...]
    halo = jnp.where(s > 0, halo_ref[...], 0.0)
    ext = jnp.concatenate([halo, cur], axis=0)

    pos = s * ts + lax.broadcasted_iota(jnp.int32, (ts, POOL_GROUP), 0)
    pool_out = []
    for g, w in enumerate(POOL_WINDOWS):
        xg = ext[:, g * POOL_GROUP:(g + 1) * POOL_GROUP]
        acc = xg
        step = 1
        while step < w:
            acc = acc + pltpu.roll(acc, step, axis=0)
            step *= 2
        count = jnp.minimum(pos + 1, w).astype(F32)
        pooled = acc[HALO:] / count - xg[HALO:]
        pool_out.append(jnp.dot(pooled.astype(BF16), wpool_ref[g], preferred_element_type=F32))
    y_pool = jnp.concatenate(pool_out, axis=1) * pscale_ref[...]

    xv = ext[:, POOL_WIDTH:POOL_WIDTH + CONV_WIDTH]
    xg = ext[:, POOL_WIDTH + CONV_WIDTH:]
    hglu = xv * jax.nn.sigmoid(xg)
    acc = jnp.zeros((ts, CONV_WIDTH), F32)
    for k in range(CONV_KERNEL):
        delay = CONV_KERNEL - 1 - k
        shifted = hglu if delay == 0 else pltpu.roll(hglu, delay, axis=0)
        acc = acc + dww_ref[k:k + 1, :] * shifted[HALO:]
    hc = _layer_norm(acc + dwb_ref[...], cng_ref[...], cnb_ref[...])
    y_conv = hc * jax.nn.sigmoid(hc)
    o_ref[...] = jnp.concatenate([y_pool, y_conv], axis=1).astype(o_ref.dtype)


def _mixer(hpc, wpool, pscale, dww, dwb, cng, cnb, batch, seq, ts):
    hpc3 = hpc.reshape(batch, seq, PC_WIDTH)
    hb = ts // HALO
    out = pl.pallas_call(
        functools.partial(_mixer_kernel, ts=ts),
        grid=(batch, seq // ts),
        in_specs=[pl.BlockSpec((None, ts, PC_WIDTH), lambda b, s: (b, s, 0)),
                  pl.BlockSpec((None, HALO, PC_WIDTH), lambda b, s: (b, jnp.maximum(s * hb - 1, 0), 0)),
                  pl.BlockSpec((4, POOL_GROUP, POOL_GROUP), lambda b, s: (0, 0, 0)),
                  pl.BlockSpec((1, POOL_WIDTH), lambda b, s: (0, 0)),
                  pl.BlockSpec((CONV_KERNEL, CONV_WIDTH), lambda b, s: (0, 0)),
                  pl.BlockSpec((1, CONV_WIDTH), lambda b, s: (0, 0)),
                  pl.BlockSpec((1, CONV_WIDTH), lambda b, s: (0, 0)),
                  pl.BlockSpec((1, CONV_WIDTH), lambda b, s: (0, 0))],
        out_specs=pl.BlockSpec((None, ts, POOL_WIDTH + CONV_WIDTH), lambda b, s: (b, s, 0)),
        out_shape=jax.ShapeDtypeStruct((batch, seq, POOL_WIDTH + CONV_WIDTH), BF16),
        compiler_params=_cparams(2),
        name="pool_conv_mixer",
    )(hpc3, hpc3, wpool, pscale, dww, dwb, cng, cnb)
    return out.reshape(batch * seq, POOL_WIDTH + CONV_WIDTH)


def _attn_kernel(q_ref, k0_ref, k1_ref, k2_ref, v0_ref, v1_ref, v2_ref, bias_ref, o_ref):
    i = pl.program_id(1)
    k_refs = (k0_ref, k1_ref, k2_ref)
    v_refs = (v0_ref, v1_ref, v2_ref)
    scale = HEAD_DIM ** -0.5
    for h in range(ATTN_HEADS):
        cols = slice(h * HEAD_DIM, (h + 1) * HEAD_DIM)
        q = q_ref[:, cols]
        scores = []
        for r in range(ATTN_KBLKS):
            sc = lax.dot_general(q, k_refs[r][:, cols], (((1,), (1,)), ((), ())),
                                 preferred_element_type=F32)
            sc = sc * scale + bias_ref[h, :, r * ATTN_TQ:(r + 1) * ATTN_TQ]
            scores.append(jnp.where(i - (ATTN_KBLKS - 1) + r >= 0, sc, NEG_BIG))
        m = scores[0].max(axis=1, keepdims=True)
        for r in range(1, ATTN_KBLKS):
            m = jnp.maximum(m, scores[r].max(axis=1, keepdims=True))
        denom = jnp.zeros_like(m)
        acc = jnp.zeros((ATTN_TQ, HEAD_DIM), F32)
        for r in range(ATTN_KBLKS):
            e = jnp.exp(scores[r] - m)
            denom = denom + e.sum(axis=1, keepdims=True)
            acc = acc + jnp.dot(e.astype(BF16), v_refs[r][:, cols], preferred_element_type=F32)
        o_ref[:, cols] = (acc / denom).astype(o_ref.dtype)


def _attention(qkv, bias_tile, batch, seq):
    qkv3 = qkv.reshape(batch, seq, 3 * ATTN_WIDTH)
    blk = (None, ATTN_TQ, ATTN_WIDTH)

    def kv_spec(col, r):
        return pl.BlockSpec(blk, lambda b, i: (b, jnp.maximum(i - (ATTN_KBLKS - 1) + r, 0), col))

    out = pl.pallas_call(
        _attn_kernel,
        grid=(batch, seq // ATTN_TQ),
        in_specs=[pl.BlockSpec(blk, lambda b, i: (b, i, 0))]
                 + [kv_spec(1, r) for r in range(ATTN_KBLKS)]
                 + [kv_spec(2, r) for r in range(ATTN_KBLKS)]
                 + [pl.BlockSpec((ATTN_HEADS, ATTN_TQ, ATTN_KBLKS * ATTN_TQ), lambda b, i: (0, 0, 0))],
        out_specs=pl.BlockSpec(blk, lambda b, i: (b, i, 0)),
        out_shape=jax.ShapeDtypeStruct((batch, seq, ATTN_WIDTH), BF16),
        compiler_params=_cparams(2),
        name="chunk_attention",
    )(qkv3, qkv3, qkv3, qkv3, qkv3, qkv3, qkv3, bias_tile)
    return out.reshape(batch * seq, ATTN_WIDTH)


def _attn_bias_tile(rel_table):
    nq, nk = ATTN_TQ, ATTN_KBLKS * ATTN_TQ
    period = nq + nk
    m = np.arange(period)
    offset = np.where(m < nk, m, m - period)
    dist = (ATTN_KBLKS - 1) * ATTN_TQ - offset
    rel_idx = np.clip(dist, -REL_RIGHT, REL_LEFT) + REL_RIGHT
    per_offset = rel_table[:, rel_idx].astype(F32)
    toe = jnp.tile(per_offset, (1, nq))[:, :nq * (period - 1)]
    toe = toe.reshape(ATTN_HEADS, nq, period - 1)[:, :, :nk]
    qpos = np.arange(nq)[:, None] + (ATTN_KBLKS - 1) * ATTN_TQ
    kpos = np.arange(nk)[None, :]
    chunk_start = (qpos // CHUNK) * CHUNK
    in_band = (kpos >= chunk_start - LEFT_CHUNKS * CHUNK) & (kpos < chunk_start + CHUNK)
    return jnp.where(in_band[None], toe, NEG_BIG)


def _outproj_kernel(ypc_ref, ya_ref, w_ref, x_ref, g_ref, b_ref, x1_ref, x1t_ref):
    half = POOL_WIDTH + CONV_WIDTH
    mixed = jnp.dot(ypc_ref[...], w_ref[:half, :], preferred_element_type=F32)
    mixed = mixed + jnp.dot(ya_ref[...], w_ref[half:, :], preferred_element_type=F32)
    y = _layer_norm(DEEPNORM_ALPHA * x_ref[...] + mixed, g_ref[...], b_ref[...])
    x1_ref[...] = y
    x1t_ref[...] = y.T.astype(BF16)


def _outproj(ypc, ya, w_out_b, x, g, b, tm):
    t = x.shape[0]
    return pl.pallas_call(
        _outproj_kernel,
        grid=(t // tm,),
        in_specs=[pl.BlockSpec((tm, POOL_WIDTH + CONV_WIDTH), lambda i: (i, 0)),
                  pl.BlockSpec((tm, ATTN_WIDTH), lambda i: (i, 0)),
                  pl.BlockSpec((D_MODEL, D_MODEL), lambda i: (0, 0)),
                  pl.BlockSpec((tm, D_MODEL), lambda i: (i, 0)),
                  pl.BlockSpec((1, D_MODEL), lambda i: (0, 0)),
                  pl.BlockSpec((1, D_MODEL), lambda i: (0, 0))],
        out_specs=[pl.BlockSpec((tm, D_MODEL), lambda i: (i, 0)),
                   pl.BlockSpec((D_MODEL, tm), lambda i: (0, i))],
        out_shape=[jax.ShapeDtypeStruct((t, D_MODEL), F32),
                   jax.ShapeDtypeStruct((D_MODEL, t), BF16)],
        compiler_params=_cparams(1),
        name="outproj_ln1",
    )(ypc, ya, w_out_b, x, g, b)


def _extract_topk(cur, order, k):
    rank = jnp.full(cur.shape, float(k), F32)
    vals = []
    for r in range(k):
        m = jnp.max(cur, axis=0, keepdims=True)
        first = jnp.min(jnp.where(cur == m, order, 1e9), axis=0, keepdims=True)
        hit = order == first
        rank = jnp.where(hit, float(r), rank)
        cur = jnp.where(hit, -jnp.inf, cur)
        vals.append(m)
    return vals, rank


_MID_K2 = tuple(range(1, 8))


def _route_kernel(xt_ref, wqt_ref, keys_ref, n_ref, ea_ref, rb_ref, eb_ref, qt_scr, s_scr, *, tp):
    qt_scr[...] = jnp.dot(wqt_ref[...], xt_ref[...], preferred_element_type=F32)
    key_order = lax.broadcasted_iota(jnp.int32, (PEER_NKEYS, tp), 0).astype(F32)
    row8 = lax.broadcasted_iota(jnp.int32, (SUBLANES, tp), 0).astype(F32)
    row16 = lax.broadcasted_iota(jnp.int32, (2 * SUBLANES, tp), 0).astype(F32)

    def head(h, carry):
        base = pl.multiple_of(h * 2 * PEER_HALF, 2 * PEER_HALF)
        sc, ranks = [], []
        for p in range(2):
            qh = qt_scr[pl.ds(base + p * PEER_HALF, PEER_HALF), :].astype(BF16)
            s = jnp.dot(keys_ref[p], qh, preferred_element_type=F32)
            vals, rank = _extract_topk(s, key_order, PEER_TOPK)
            for r in range(PEER_TOPK):
                s_scr[p, r:r + 1, :] = vals[r]
            sc.append(s)
            ranks.append(rank)
        s1 = s_scr[0]
        s2 = s_scr[1]
        cands = [s1 + s2[0:1, :]]
        order = [row16 * PEER_TOPK]
        for k2 in _MID_K2:
            valid = row8 < float(PEER_TOPK // (k2 + 1))
            cands.append(jnp.where(valid, s1[:SUBLANES] + s2[k2:k2 + 1, :], -jnp.inf))
            order.append(row8 * PEER_TOPK + float(k2))
        cands.append(s1[0:1, :] + s2[SUBLANES:])
        order.append(row8 + float(SUBLANES))
        cand = jnp.concatenate(cands, axis=0)
        cvals, crank = _extract_topk(cand, jnp.concatenate(order, axis=0), PEER_TOPK)
        sel = crank < float(PEER_TOPK)
        self_ = sel.astype(F32)
        top = s1[0:1, :] + s2[0:1, :]
        z = jnp.sum(jnp.where(sel, jnp.exp(cand - top), 0.0), axis=0, keepdims=True)
        n_lo = self_[0:SUBLANES]
        for g in range(len(_MID_K2)):
            lo = 2 * SUBLANES + g * SUBLANES
            n_lo = n_lo + self_[lo:lo + SUBLANES]
        tail = jnp.sum(self_[2 * SUBLANES + len(_MID_K2) * SUBLANES:], axis=0, keepdims=True)
        n_lo = n_lo + jnp.where(row8 == 0.0, tail, 0.0)
        n16 = jnp.concatenate([n_lo, self_[SUBLANES:2 * SUBLANES]], axis=0)
        nkey = jnp.zeros((PEER_NKEYS, tp), F32)
        for k1 in range(PEER_TOPK):
            nkey = jnp.where(ranks[0] == float(k1), n16[k1:k1 + 1, :], nkey)
        n_ref[h] = nkey
        ea_ref[h] = jnp.exp(sc[0] - s1[0:1, :]) / z
        rb_ref[h] = ranks[1].astype(BF16)
        eb_ref[h] = jnp.exp(sc[1] - s2[0:1, :]).astype(BF16)
        return carry

    lax.fori_loop(0, PEER_HEADS, head, 0)


def _route(x1t, wqt_b, keys_b, tp):
    t = x1t.shape[1]
    gate_spec = pl.BlockSpec((PEER_HEADS, PEER_NKEYS, tp), lambda i: (0, 0, i))
    gate_shapes = [jax.ShapeDtypeStruct((PEER_HEADS, PEER_NKEYS, t), dt) for dt in (F32, F32, BF16, BF16)]
    return pl.pallas_call(
        functools.partial(_route_kernel, tp=tp),
        grid=(t // tp,),
        in_specs=[pl.BlockSpec((D_MODEL, tp), lambda i: (0, i)),
                  pl.BlockSpec((D_MODEL, D_MODEL), lambda i: (0, 0)),
                  pl.BlockSpec((2, PEER_NKEYS, PEER_HALF), lambda i: (0, 0, 0))],
        out_specs=[gate_spec] * 4,
        out_shape=gate_shapes,
        scratch_shapes=[pltpu.VMEM((D_MODEL, tp), F32),
                        pltpu.VMEM((2, PEER_TOPK, tp), F32)],
        compiler_params=_cparams(1),
        name="peer_route",
    )(x1t, wqt_b, keys_b)


PEER_TE = 512
PEER_NJ = PEER_EXPERTS // PEER_TE
PEER_GROUPS = PEER_TE // PEER_NKEYS
PEER_ROWS = 16
PEER_LANES = 256
PEER_D1_ROWS = 512
PEER_D2_ROWS = 128


def _experts_tick(p, j, xt_ref, u_ref, vt_ref, n_ref, ea_ref, rb_ref, eb_ref, y_ref, h_scr, a_scr, tm):
    live = jnp.logical_and(j >= 1, j <= PEER_NJ).astype(F32)
    blk = jnp.clip(j - 1, 0, PEER_NJ - 1)

    def gate_chunk(ii, lt):
        i = blk * PEER_GROUPS + ii
        lanes = slice(lt * PEER_LANES, (lt + 1) * PEER_LANES)
        shape = (PEER_ROWS, PEER_LANES)
        n_rows = [jnp.broadcast_to(n_ref[h, pl.ds(i, 1), :][:, lanes].astype(BF16), shape)
                  for h in range(PEER_HEADS)]
        ea_rows = [jnp.broadcast_to((ea_ref[h, pl.ds(i, 1), :][:, lanes] * live).astype(BF16), shape)
                   for h in range(PEER_HEADS)]
        zero = jnp.zeros(shape, BF16)
        for r0 in range(0, PEER_NKEYS, PEER_ROWS):
            gate = None
            for h in range(PEER_HEADS):
                term = jnp.where(rb_ref[h, r0:r0 + PEER_ROWS, lanes] < n_rows[h],
                                 eb_ref[h, r0:r0 + PEER_ROWS, lanes], zero) * ea_rows[h]
                gate = term if gate is None else gate + term
            rows = slice(ii * PEER_NKEYS + r0, ii * PEER_NKEYS + r0 + PEER_ROWS)
            hh = h_scr[1 - p, rows, lanes]
            act = 0.5 * hh * (1.0 + lax.erf(hh * SQRT_HALF))
            a_scr[1 - p, rows, lanes] = act.astype(BF16) * gate

    chunks = [(ii, lt) for ii in range(PEER_GROUPS) for lt in range(tm // PEER_LANES)]
    d1_pieces = list(range(0, PEER_TE, PEER_D1_ROWS))
    d2_pieces = list(range(0, D_MODEL, PEER_D2_ROWS))
    n1, n2 = len(d1_pieces), len(d2_pieces)
    for s, (ii, lt) in enumerate(chunks):
        for er in d1_pieces[s * n1 // len(chunks):(s + 1) * n1 // len(chunks)]:
            h_scr[p, er:er + PEER_D1_ROWS, :] = jnp.dot(
                u_ref[er:er + PEER_D1_ROWS, :], xt_ref[...], preferred_element_type=F32)
        gate_chunk(ii, lt)
        for dr in d2_pieces[s * n2 // len(chunks):(s + 1) * n2 // len(chunks)]:
            y_ref[dr:dr + PEER_D2_ROWS, :] += jnp.dot(
                vt_ref[dr:dr + PEER_D2_ROWS, :], a_scr[p], preferred_element_type=F32)


def _experts_kernel(xt_ref, u_ref, vt_ref, n_ref, ea_ref, rb_ref, eb_ref, y_ref, h_scr, a_scr, *, tm):
    i = pl.program_id(0)
    j = pl.program_id(1)

    @pl.when(jnp.logical_and(i == 0, j == 0))
    def _():
        h_scr[...] = jnp.zeros_like(h_scr)
        a_scr[...] = jnp.zeros_like(a_scr)

    @pl.when(j == 0)
    def _():
        y_ref[...] = jnp.zeros_like(y_ref)

    args = (xt_ref, u_ref, vt_ref, n_ref, ea_ref, rb_ref, eb_ref, y_ref, h_scr, a_scr, tm)

    @pl.when(j % 2 == 0)
    def _():
        _experts_tick(0, j, *args)

    @pl.when(j % 2 == 1)
    def _():
        _experts_tick(1, j, *args)


def _experts(x1t, u_b, vt_b, gates, tm):
    t = x1t.shape[1]
    assert PEER_NJ % 2 == 0
    gate_spec = pl.BlockSpec((PEER_HEADS, PEER_NKEYS, tm), lambda i, j: (0, 0, i))
    return pl.pallas_call(
        functools.partial(_experts_kernel, tm=tm),
        grid=(t // tm, PEER_NJ + 2),
        in_specs=[pl.BlockSpec((D_MODEL, tm), lambda i, j: (0, i)),
                  pl.BlockSpec((PEER_TE, D_MODEL), lambda i, j: (jnp.minimum(j, PEER_NJ - 1), 0)),
                  pl.BlockSpec((D_MODEL, PEER_TE), lambda i, j: (0, jnp.clip(j - 2, 0, PEER_NJ - 1)))]
                 + [gate_spec] * 4,
        out_specs=pl.BlockSpec((D_MODEL, tm), lambda i, j: (0, i)),
        out_shape=jax.ShapeDtypeStruct((D_MODEL, t), F32),
        scratch_shapes=[pltpu.VMEM((2, PEER_TE, tm), F32),
                        pltpu.VMEM((2, PEER_TE, tm), BF16)],
        compiler_params=_cparams(2),
        name="peer_experts",
    )(x1t, u_b, vt_b, *gates)


def _final_kernel(x1_ref, yt_ref, p_ref, wple_ref, wpg_ref, g_ref, b_ref, x2_ref, x2b_ref):
    x1 = x1_ref[...]
    emb = jnp.dot(p_ref[...].astype(BF16), wple_ref[...], preferred_element_type=F32)
    gate = jax.nn.sigmoid(jnp.dot(x1.astype(BF16), wpg_ref[...], preferred_element_type=F32))
    z = DEEPNORM_ALPHA * x1 + yt_ref[...].T + emb * gate
    y = _layer_norm(z, g_ref[...], b_ref[...])
    x2_ref[...] = y
    x2b_ref[...] = y.astype(BF16)


def _final(x1, yt, p, wple_b, wpg_b, g, b, tm):
    t = x1.shape[0]
    pdim = p.shape[1]
    return pl.pallas_call(
        _final_kernel,
        grid=(t // tm,),
        in_specs=[pl.BlockSpec((tm, D_MODEL), lambda i: (i, 0)),
                  pl.BlockSpec((D_MODEL, tm), lambda i: (0, i)),
                  pl.BlockSpec((tm, pdim), lambda i: (i, 0)),
                  pl.BlockSpec((pdim, D_MODEL), lambda i: (0, 0)),
                  pl.BlockSpec((D_MODEL, D_MODEL), lambda i: (0, 0)),
                  pl.BlockSpec((1, D_MODEL), lambda i: (0, 0)),
                  pl.BlockSpec((1, D_MODEL), lambda i: (0, 0))],
        out_specs=[pl.BlockSpec((tm, D_MODEL), lambda i: (i, 0)),
                   pl.BlockSpec((tm, D_MODEL), lambda i: (i, 0))],
        out_shape=[jax.ShapeDtypeStruct((t, D_MODEL), F32),
                   jax.ShapeDtypeStruct((t, D_MODEL), BF16)],
        compiler_params=_cparams(1),
        name="ple_ln2",
    )(x1, yt, p, wple_b, wpg_b, g, b)


def _tiles(batch, seq):
    t = batch * seq
    return dict(
        mm_tm=min(512, t), mm_tn=1536,
        mixer_ts=min(512, seq),
        out_tm=min(256, t),
        route_tp=min(256, t),
        peer_tm=min(512, t),
        final_tm=min(256, t),
    )


def kernel(x, p, w_in, b_in, w_pool, pool_scale, dw_w, dw_b, cn_g, cn_b, rel_bias, w_out, ln1_g, ln1_b,
           w_q, sub_keys, u_tab, v_tab, w_ple, w_pg, ln2_g, ln2_b):
    batch, seq, d = x.shape
    depth = w_in.shape[0]
    t = batch * seq
    assert d == D_MODEL and depth == DEPTH and seq % ATTN_TQ == 0
    tl = _tiles(batch, seq)

    xf = x.reshape(t, d)
    xb = xf.astype(BF16)
    for i in range(depth):
        w_in_b = w_in[i].astype(BF16)
        b_row = b_in[i].reshape(1, -1)
        hpc = _matmul_bias(xb, w_in_b[:, :PC_WIDTH], b_row[:, :PC_WIDTH], F32,
                           tl["mm_tm"], PC_WIDTH, "inproj_pool_conv")
        qkv = _matmul_bias(xb, w_in_b[:, PC_WIDTH:], b_row[:, PC_WIDTH:], BF16,
                           tl["mm_tm"], tl["mm_tn"], "inproj_qkv")
        ypc = _mixer(hpc, w_pool[i].astype(BF16), pool_scale[i].reshape(1, -1),
                     dw_w[i].reshape(CONV_KERNEL, CONV_WIDTH), dw_b[i].reshape(1, -1),
                     cn_g[i].reshape(1, -1), cn_b[i].reshape(1, -1), batch, seq, tl["mixer_ts"])
        ya = _attention(qkv, _attn_bias_tile(rel_bias[i]), batch, seq)
        x1, x1t = _outproj(ypc, ya, w_out[i].astype(BF16), xf, ln1_g[i].reshape(1, -1),
                           ln1_b[i].reshape(1, -1), tl["out_tm"])
        gates = _route(x1t, w_q[i].T.astype(BF16), sub_keys[i].astype(BF16), tl["route_tp"])
        yt = _experts(x1t, u_tab[i].astype(BF16), v_tab[i].T.astype(BF16), gates, tl["peer_tm"])
        xf, xb = _final(x1, yt, p[i].reshape(t, -1), w_ple[i].astype(BF16), w_pg[i].astype(BF16),
                        ln2_g[i].reshape(1, -1), ln2_b[i].reshape(1, -1), tl["final_tm"])
    return xf.reshape(batch, seq, d)
```

```python
import functools

import jax
import jax.numpy as jnp
import numpy as np
from jax import lax
from jax.experimental import pallas as pl
from jax.experimental.pallas import tpu as pltpu

F32 = jnp.float32
BF16 = jnp.bfloat16

D_MODEL = 2048
CHUNK = 64
POOL_WINDOWS = (2, 4, 8, 16)
POOL_GROUP = 128
POOL_WIDTH = 512
CONV_WIDTH = 512
CONV_KERNEL = 31
ATTN_WIDTH = 1024
ATTN_HEADS = 8
HEAD_DIM = 128
LEFT_CHUNKS = 8
BAND = (LEFT_CHUNKS + 1) * CHUNK
REL_LEFT = 128
REL_RIGHT = CHUNK - 1
PC_WIDTH = POOL_WIDTH + 2 * CONV_WIDTH
PEER_HEADS = 8
PEER_NKEYS = 128
PEER_EXPERTS = PEER_NKEYS * PEER_NKEYS
PEER_HALF = 128
PEER_TOPK = 16
LN_EPS = 1e-5
DEPTH = 4
DEEPNORM_ALPHA = (2 * DEPTH) ** 0.25

LANES = 128
SUBLANES = 8
VMEM_LIMIT = 56 * 1024 * 1024

HALO = 32
ATTN_TQ = 256
ATTN_KBLKS = 3
NEG_BIG = -1e30
SQRT_HALF = 0.7071067811865476


def _cparams(n_axes, flags=None):
    return pltpu.CompilerParams(dimension_semantics=("arbitrary",) * n_axes,
                                vmem_limit_bytes=VMEM_LIMIT, flags=flags)


def _layer_norm(z, g, b):
    mu = jnp.mean(z, axis=-1, keepdims=True)
    zc = z - mu
    var = jnp.mean(zc * zc, axis=-1, keepdims=True)
    return zc * lax.rsqrt(var + LN_EPS) * g + b


def _mm_bias_kernel(x_ref, w_ref, b_ref, o_ref):
    acc = jnp.dot(x_ref[...], w_ref[...], preferred_element_type=F32)
    o_ref[...] = (acc + b_ref[...]).astype(o_ref.dtype)


def _matmul_bias(xb, w, b, out_dtype, tm, tn, col0, n, name):
    t, k = xb.shape
    assert col0 % tn == 0 and n % tn == 0
    first = col0 // tn
    return pl.pallas_call(
        _mm_bias_kernel,
        grid=(n // tn, t // tm),
        in_specs=[pl.BlockSpec((tm, k), lambda j, i: (i, 0)),
                  pl.BlockSpec((k, tn), lambda j, i: (0, first + j)),
                  pl.BlockSpec((1, tn), lambda j, i: (0, first + j))],
        out_specs=pl.BlockSpec((tm, tn), lambda j, i: (i, j)),
        out_shape=jax.ShapeDtypeStruct((t, n), out_dtype),
        compiler_params=_cparams(2),
        name=name,
    )(xb, w, b)


def _mixer_kernel(cur_ref, halo_ref, wpool_ref, pscale_ref, dww_ref, dwb_ref, cng_ref, cnb_ref,
                  o_ref, *, ts):
    s = pl.program_id(1)
    cur = cur_ref[...]
    halo = jnp.where(s > 0, halo_ref[...], 0.0)
    ext = jnp.concatenate([halo, cur], axis=0)

    pos = s * ts + lax.broadcasted_iota(jnp.int32, (ts, POOL_GROUP), 0)
    pool_out = []
    for g, w in enumerate(POOL_WINDOWS):
        xg = ext[:, g * POOL_GROUP:(g + 1) * POOL_GROUP]
        acc = xg
        step = 1
        while step < w:
            acc = acc + pltpu.roll(acc, step, axis=0)
            step *= 2
        count = jnp.minimum(pos + 1, w).astype(F32)
        pooled = acc[HALO:] / count - xg[HALO:]
        pool_out.append(jnp.dot(pooled.astype(BF16), wpool_ref[g], preferred_element_type=F32))
    y_pool = jnp.concatenate(pool_out, axis=1) * pscale_ref[...]

    xv = ext[:, POOL_WIDTH:POOL_WIDTH + CONV_WIDTH]
    xg = ext[:, POOL_WIDTH + CONV_WIDTH:]
    hglu = xv * jax.nn.sigmoid(xg)
    acc = jnp.zeros((ts, CONV_WIDTH), F32)
    for k in range(CONV_KERNEL):
        delay = CONV_KERNEL - 1 - k
        shifted = hglu if delay == 0 else pltpu.roll(hglu, delay, axis=0)
        acc = acc + dww_ref[k:k + 1, :] * shifted[HALO:]
    hc = _layer_norm(acc + dwb_ref[...], cng_ref[...], cnb_ref[...])
    y_conv = hc * jax.nn.sigmoid(hc)
    o_ref[...] = jnp.concatenate([y_pool, y_conv], axis=1).astype(o_ref.dtype)


def _mixer(hpc, wpool, pscale, dww, dwb, cng, cnb, batch, seq, ts):
    hpc3 = hpc.reshape(batch, seq, PC_WIDTH)
    hb = ts // HALO
    out = pl.pallas_call(
        functools.partial(_mixer_kernel, ts=ts),
        grid=(batch, seq // ts),
        in_specs=[pl.BlockSpec((None, ts, PC_WIDTH), lambda b, s: (b, s, 0)),
                  pl.BlockSpec((None, HALO, PC_WIDTH), lambda b, s: (b, jnp.maximum(s * hb - 1, 0), 0)),
                  pl.BlockSpec((4, POOL_GROUP, POOL_GROUP), lambda b, s: (0, 0, 0)),
                  pl.BlockSpec((1, POOL_WIDTH), lambda b, s: (0, 0)),
                  pl.BlockSpec((CONV_KERNEL, CONV_WIDTH), lambda b, s: (0, 0)),
                  pl.BlockSpec((1, CONV_WIDTH), lambda b, s: (0, 0)),
                  pl.BlockSpec((1, CONV_WIDTH), lambda b, s: (0, 0)),
                  pl.BlockSpec((1, CONV_WIDTH), lambda b, s: (0, 0))],
        out_specs=pl.BlockSpec((None, ts, POOL_WIDTH + CONV_WIDTH), lambda b, s: (b, s, 0)),
        out_shape=jax.ShapeDtypeStruct((batch, seq, POOL_WIDTH + CONV_WIDTH), BF16),
        compiler_params=_cparams(2),
        name="pool_conv_mixer",
    )(hpc3, hpc3, wpool, pscale, dww, dwb, cng, cnb)
    return out.reshape(batch * seq, POOL_WIDTH + CONV_WIDTH)


def _attn_kernel(q_ref, k0_ref, k1_ref, k2_ref, v0_ref, v1_ref, v2_ref, bias_ref, o_ref):
    i = pl.program_id(1)
    k_refs = (k0_ref, k1_ref, k2_ref)
    v_refs = (v0_ref, v1_ref, v2_ref)
    scale = HEAD_DIM ** -0.5
    for h in range(ATTN_HEADS):
        cols = slice(h * HEAD_DIM, (h + 1) * HEAD_DIM)
        q = q_ref[:, cols]
        scores = []
        for r in range(ATTN_KBLKS):
            sc = lax.dot_general(q, k_refs[r][:, cols], (((1,), (1,)), ((), ())),
                                 preferred_element_type=F32)
            sc = sc * scale + bias_ref[h, :, r * ATTN_TQ:(r + 1) * ATTN_TQ]
            scores.append(jnp.where(i - (ATTN_KBLKS - 1) + r >= 0, sc, NEG_BIG))
        m = scores[0].max(axis=1, keepdims=True)
        for r in range(1, ATTN_KBLKS):
            m = jnp.maximum(m, scores[r].max(axis=1, keepdims=True))
        denom = jnp.zeros_like(m)
        acc = jnp.zeros((ATTN_TQ, HEAD_DIM), F32)
        for r in range(ATTN_KBLKS):
            e = jnp.exp(scores[r] - m)
            denom = denom + e.sum(axis=1, keepdims=True)
            acc = acc + jnp.dot(e.astype(BF16), v_refs[r][:, cols], preferred_element_type=F32)
        o_ref[:, cols] = (acc / denom).astype(o_ref.dtype)


def _attention(qkv, bias_tile, batch, seq):
    qkv3 = qkv.reshape(batch, seq, 3 * ATTN_WIDTH)
    blk = (None, ATTN_TQ, ATTN_WIDTH)

    def kv_spec(col, r):
        return pl.BlockSpec(blk, lambda b, i: (b, jnp.maximum(i - (ATTN_KBLKS - 1) + r, 0), col))

    out = pl.pallas_call(
        _attn_kernel,
        grid=(batch, seq // ATTN_TQ),
        in_specs=[pl.BlockSpec(blk, lambda b, i: (b, i, 0))]
                 + [kv_spec(1, r) for r in range(ATTN_KBLKS)]
                 + [kv_spec(2, r) for r in range(ATTN_KBLKS)]
                 + [pl.BlockSpec((ATTN_HEADS, ATTN_TQ, ATTN_KBLKS * ATTN_TQ), lambda b, i: (0, 0, 0))],
        out_specs=pl.BlockSpec(blk, lambda b, i: (b, i, 0)),
        out_shape=jax.ShapeDtypeStruct((batch, seq, ATTN_WIDTH), BF16),
        compiler_params=_cparams(2),
        name="chunk_attention",
    )(qkv3, qkv3, qkv3, qkv3, qkv3, qkv3, qkv3, bias_tile)
    return out.reshape(batch * seq, ATTN_WIDTH)


def _attn_bias_tile(rel_table):
    nq, nk = ATTN_TQ, ATTN_KBLKS * ATTN_TQ
    period = nq + nk
    m = np.arange(period)
    offset = np.where(m < nk, m, m - period)
    dist = (ATTN_KBLKS - 1) * ATTN_TQ - offset
    rel_idx = np.clip(dist, -REL_RIGHT, REL_LEFT) + REL_RIGHT
    per_offset = rel_table[:, rel_idx].astype(F32)
    toe = jnp.tile(per_offset, (1, nq))[:, :nq * (period - 1)]
    toe = toe.reshape(ATTN_HEADS, nq, period - 1)[:, :, :nk]
    qpos = np.arange(nq)[:, None] + (ATTN_KBLKS - 1) * ATTN_TQ
    kpos = np.arange(nk)[None, :]
    chunk_start = (qpos // CHUNK) * CHUNK
    in_band = (kpos >= chunk_start - LEFT_CHUNKS * CHUNK) & (kpos < chunk_start + CHUNK)
    return jnp.where(in_band[None], toe, NEG_BIG)


def _outproj_kernel(ypc_ref, ya_ref, w_ref, x_ref, g_ref, b_ref, x1_ref, x1t_ref):
    half = POOL_WIDTH + CONV_WIDTH
    mixed = jnp.dot(ypc_ref[...], w_ref[:half, :], preferred_element_type=F32)
    mixed = mixed + jnp.dot(ya_ref[...], w_ref[half:, :], preferred_element_type=F32)
    y = _layer_norm(DEEPNORM_ALPHA * x_ref[...] + mixed, g_ref[...], b_ref[...])
    x1_ref[...] = y
    x1t_ref[...] = y.T.astype(BF16)


def _outproj(ypc, ya, w_out_b, x, g, b, tm):
    t = x.shape[0]
    return pl.pallas_call(
        _outproj_kernel,
        grid=(t // tm,),
        in_specs=[pl.BlockSpec((tm, POOL_WIDTH + CONV_WIDTH), lambda i: (i, 0)),
                  pl.BlockSpec((tm, ATTN_WIDTH), lambda i: (i, 0)),
                  pl.BlockSpec((D_MODEL, D_MODEL), lambda i: (0, 0)),
                  pl.BlockSpec((tm, D_MODEL), lambda i: (i, 0)),
                  pl.BlockSpec((1, D_MODEL), lambda i: (0, 0)),
                  pl.BlockSpec((1, D_MODEL), lambda i: (0, 0))],
        out_specs=[pl.BlockSpec((tm, D_MODEL), lambda i: (i, 0)),
                   pl.BlockSpec((D_MODEL, tm), lambda i: (0, i))],
        out_shape=[jax.ShapeDtypeStruct((t, D_MODEL), F32),
                   jax.ShapeDtypeStruct((D_MODEL, t), BF16)],
        compiler_params=_cparams(1),
        name="outproj_ln1",
    )(ypc, ya, w_out_b, x, g, b)


def _col_reduce(x, op):
    parts = [x[r:r + SUBLANES] for r in range(0, x.shape[0], SUBLANES)]
    while len(parts) > 1:
        parts = [op(parts[i], parts[i + 1]) if i + 1 < len(parts) else parts[i]
                 for i in range(0, len(parts), 2)]
    last = parts[0]
    if op is jnp.maximum:
        return jnp.max(last, axis=0, keepdims=True)
    if op is jnp.minimum:
        return jnp.min(last, axis=0, keepdims=True)
    return jnp.sum(last, axis=0, keepdims=True)


def _extract_topk_exact(cur, order, k):
    rank = jnp.full(cur.shape, float(k), F32)
    vals = []
    for r in range(k):
        m = _col_reduce(cur, jnp.maximum)
        first = _col_reduce(jnp.where(cur == m, order, 1e9), jnp.minimum)
        hit = order == first
        rank = jnp.where(hit, float(r), rank)
        cur = jnp.where(hit, -jnp.inf, cur)
        vals.append(m)
    return vals, rank


def _col_maxima(a, k):
    m = _col_reduce(a, jnp.maximum)
    vals = [m]
    for _ in range(1, k):
        m = _col_reduce(jnp.where(a < m, a, -jnp.inf), jnp.maximum)
        vals.append(m)
    return vals


def _has_tie(a, kth, k):
    count = _col_reduce(jnp.where(a >= kth, 1.0, 0.0), jnp.add)
    return jnp.max(count) > float(k)


def _rows_to_ref(ref, rows):
    for r, v in enumerate(rows):
        ref[r:r + 1, :] = v


_MID_K2 = tuple(range(1, 8))


N_CAND_ROWS = 2 * SUBLANES + len(_MID_K2) * SUBLANES + SUBLANES


def _route_kernel(xt_ref, wqt_ref, keys_ref, n_ref, ea_ref, rb_ref, eb_ref, qt_scr, s_scr, *, tp):
    qt_scr[...] = jnp.dot(wqt_ref[...], xt_ref[...], preferred_element_type=F32)
    key_order = lax.broadcasted_iota(jnp.int32, (PEER_NKEYS, tp), 0).astype(F32)
    row8 = lax.broadcasted_iota(jnp.int32, (SUBLANES, tp), 0).astype(F32)
    row16 = lax.broadcasted_iota(jnp.int32, (2 * SUBLANES, tp), 0).astype(F32)

    def candidates(s1, s2):
        cands = [s1 + s2[0:1, :]]
        order = [row16 * PEER_TOPK]
        for k2 in _MID_K2:
            valid = row8 < float(PEER_TOPK // (k2 + 1))
            cands.append(jnp.where(valid, s1[:SUBLANES] + s2[k2:k2 + 1, :], -jnp.inf))
            order.append(row8 * PEER_TOPK + float(k2))
        cands.append(s1[0:1, :] + s2[SUBLANES:])
        order.append(row8 + float(SUBLANES))
        return jnp.concatenate(cands, axis=0), jnp.concatenate(order, axis=0)

    def picks_per_rank(sel):
        self_ = sel.astype(F32)
        n_lo = self_[0:SUBLANES]
        for g in range(len(_MID_K2)):
            lo = 2 * SUBLANES + g * SUBLANES
            n_lo = n_lo + self_[lo:lo + SUBLANES]
        tail = jnp.sum(self_[2 * SUBLANES + len(_MID_K2) * SUBLANES:], axis=0, keepdims=True)
        n_lo = n_lo + jnp.where(row8 == 0.0, tail, 0.0)
        return jnp.concatenate([n_lo, self_[SUBLANES:2 * SUBLANES]], axis=0)

    def emit(h, sc, s1, s2, cand, sel, nkey, rank_b):
        top = s1[0:1, :] + s2[0:1, :]
        z = jnp.sum(jnp.where(sel, jnp.exp(cand - top), 0.0), axis=0, keepdims=True)
        n_ref[h] = nkey
        ea_ref[h] = jnp.exp(sc[0] - s1[0:1, :]) / z
        rb_ref[h] = rank_b.astype(BF16)
        eb_ref[h] = jnp.exp(sc[1] - s2[0:1, :]).astype(BF16)

    def head(h, carry):
        base = pl.multiple_of(h * 2 * PEER_HALF, 2 * PEER_HALF)
        sc = []
        for p in range(2):
            qh = qt_scr[pl.ds(base + p * PEER_HALF, PEER_HALF), :].astype(BF16)
            sc.append(jnp.dot(keys_ref[p], qh, preferred_element_type=F32))

        tops = [_col_maxima(sc[p], PEER_TOPK) for p in range(2)]
        tied = jnp.logical_or(_has_tie(sc[0], tops[0][-1], PEER_TOPK),
                              _has_tie(sc[1], tops[1][-1], PEER_TOPK))
        for p in range(2):
            _rows_to_ref(s_scr.at[p], tops[p])
        s1, s2 = s_scr[0], s_scr[1]
        cand, _ = candidates(s1, s2)
        kth = _col_maxima(cand, PEER_TOPK)[-1]
        sel = cand >= kth
        tied = jnp.logical_or(tied, _has_tie(cand, kth, PEER_TOPK))
        n16 = picks_per_rank(sel)
        nkey = jnp.zeros((PEER_NKEYS, tp), F32)
        rank_b = jnp.full((PEER_NKEYS, tp), float(PEER_TOPK), F32)
        for r in reversed(range(PEER_TOPK)):
            nkey = jnp.where(sc[0] == s1[r:r + 1, :], n16[r:r + 1, :], nkey)
            rank_b = jnp.where(sc[1] >= s2[r:r + 1, :], float(r), rank_b)
        emit(h, sc, s1, s2, cand, sel, nkey, rank_b)

        @pl.when(tied)
        def _():
            ranks = []
            for p in range(2):
                vals, rank = _extract_topk_exact(sc[p], key_order, PEER_TOPK)
                _rows_to_ref(s_scr.at[p], vals)
                ranks.append(rank)
            t1, t2 = s_scr[0], s_scr[1]
            tcand, order = candidates(t1, t2)
            _, crank = _extract_topk_exact(tcand, order, PEER_TOPK)
            tsel = crank < float(PEER_TOPK)
            m16 = picks_per_rank(tsel)
            mkey = jnp.zeros((PEER_NKEYS, tp), F32)
            for k1 in range(PEER_TOPK):
                mkey = jnp.where(ranks[0] == float(k1), m16[k1:k1 + 1, :], mkey)
            emit(h, sc, t1, t2, tcand, tsel, mkey, ranks[1])

        return carry

    lax.fori_loop(0, PEER_HEADS, head, 0)


def _route(x1t, wqt_b, keys_b, tp):
    t = x1t.shape[1]
    gate_spec = pl.BlockSpec((PEER_HEADS, PEER_NKEYS, tp), lambda i: (0, 0, i))
    gate_shapes = [jax.ShapeDtypeStruct((PEER_HEADS, PEER_NKEYS, t), dt) for dt in (F32, F32, BF16, BF16)]
    return pl.pallas_call(
        functools.partial(_route_kernel, tp=tp),
        grid=(t // tp,),
        in_specs=[pl.BlockSpec((D_MODEL, tp), lambda i: (0, i)),
                  pl.BlockSpec((D_MODEL, D_MODEL), lambda i: (0, 0)),
                  pl.BlockSpec((2, PEER_NKEYS, PEER_HALF), lambda i: (0, 0, 0))],
        out_specs=[gate_spec] * 4,
        out_shape=gate_shapes,
        scratch_shapes=[pltpu.VMEM((D_MODEL, tp), F32),
                        pltpu.VMEM((2, PEER_TOPK, tp), F32)],
        compiler_params=_cparams(1),
        name="peer_route",
    )(x1t, wqt_b, keys_b)


PEER_TE = 1024
PEER_NJ = PEER_EXPERTS // PEER_TE
PEER_GROUPS = PEER_TE // PEER_NKEYS
PEER_ROWS = 16
PEER_LANES = 256
PEER_D1_ROWS = 512
PEER_D2_ROWS = 128


def _experts_tick(p, g, xt_ref, u_ref, vt_ref, n_ref, ea_ref, rb_ref, eb_ref, y_ref, h_scr, a_scr, tm,
                  n_blocks):
    live = jnp.logical_and(g >= 1, g <= n_blocks).astype(F32)
    blk = jnp.clip(g - 1, 0, n_blocks - 1) % PEER_NJ

    def gate_chunk(ii, lt):
        i = blk * PEER_GROUPS + ii
        lanes = slice(lt * PEER_LANES, (lt + 1) * PEER_LANES)
        shape = (PEER_ROWS, PEER_LANES)
        n_rows = [jnp.broadcast_to(n_ref[h, pl.ds(i, 1), :][:, lanes].astype(BF16), shape)
                  for h in range(PEER_HEADS)]
        ea_rows = [jnp.broadcast_to((ea_ref[h, pl.ds(i, 1), :][:, lanes] * live).astype(BF16), shape)
                   for h in range(PEER_HEADS)]
        zero = jnp.zeros(shape, BF16)
        for r0 in range(0, PEER_NKEYS, PEER_ROWS):
            gate = None
            for h in range(PEER_HEADS):
                term = jnp.where(rb_ref[h, r0:r0 + PEER_ROWS, lanes] < n_rows[h],
                                 eb_ref[h, r0:r0 + PEER_ROWS, lanes], zero) * ea_rows[h]
                gate = term if gate is None else gate + term
            rows = slice(ii * PEER_NKEYS + r0, ii * PEER_NKEYS + r0 + PEER_ROWS)
            hh = h_scr[1 - p, rows, lanes]
            act = 0.5 * hh * (1.0 + lax.erf(hh * SQRT_HALF))
            a_scr[1 - p, rows, lanes] = act.astype(BF16) * gate

    chunks = [(ii, lt) for ii in range(PEER_GROUPS) for lt in range(tm // PEER_LANES)]
    d1_pieces = list(range(0, PEER_TE, PEER_D1_ROWS))
    d2_pieces = list(range(0, D_MODEL, PEER_D2_ROWS))
    n1, n2 = len(d1_pieces), len(d2_pieces)
    for s, (ii, lt) in enumerate(chunks):
        for er in d1_pieces[s * n1 // len(chunks):(s + 1) * n1 // len(chunks)]:
            h_scr[p, er:er + PEER_D1_ROWS, :] = jnp.dot(
                u_ref[er:er + PEER_D1_ROWS, :], xt_ref[...], preferred_element_type=F32)
        gate_chunk(ii, lt)
        for dr in d2_pieces[s * n2 // len(chunks):(s + 1) * n2 // len(chunks)]:
            y_ref[dr:dr + PEER_D2_ROWS, :] += jnp.dot(
                vt_ref[dr:dr + PEER_D2_ROWS, :], a_scr[p], preferred_element_type=F32)


def _experts_kernel(xt_ref, u_ref, vt_ref, n_ref, ea_ref, rb_ref, eb_ref, y_ref, h_scr, a_scr, *,
                    tm, n_blocks):
    g = pl.program_id(0)

    @pl.when(g == 0)
    def _():
        h_scr[...] = jnp.zeros_like(h_scr)
        a_scr[...] = jnp.zeros_like(a_scr)

    @pl.when(jnp.clip(g - 2, 0, n_blocks - 1) % PEER_NJ == 0)
    def _():
        y_ref[...] = jnp.zeros_like(y_ref)

    args = (xt_ref, u_ref, vt_ref, n_ref, ea_ref, rb_ref, eb_ref, y_ref, h_scr, a_scr, tm, n_blocks)

    @pl.when(g % 2 == 0)
    def _():
        _experts_tick(0, g, *args)

    @pl.when(g % 2 == 1)
    def _():
        _experts_tick(1, g, *args)


def _experts(x1t, u_b, vt_blocks, gates, tm):
    t = x1t.shape[1]
    n_blocks = (t // tm) * PEER_NJ

    def stage(lag):
        return lambda g: jnp.clip(g - lag, 0, n_blocks - 1)

    gate_spec = pl.BlockSpec((PEER_HEADS, PEER_NKEYS, tm), lambda g: (0, 0, stage(1)(g) // PEER_NJ))
    return pl.pallas_call(
        functools.partial(_experts_kernel, tm=tm, n_blocks=n_blocks),
        grid=(n_blocks + 2,),
        in_specs=[pl.BlockSpec((D_MODEL, tm), lambda g: (0, stage(0)(g) // PEER_NJ)),
                  pl.BlockSpec((PEER_TE, D_MODEL), lambda g: (stage(0)(g) % PEER_NJ, 0)),
                  pl.BlockSpec((None, D_MODEL, PEER_TE), lambda g: (stage(2)(g) % PEER_NJ, 0, 0))]
                 + [gate_spec] * 4,
        out_specs=pl.BlockSpec((D_MODEL, tm), lambda g: (0, stage(2)(g) // PEER_NJ)),
        out_shape=jax.ShapeDtypeStruct((D_MODEL, t), F32),
        scratch_shapes=[pltpu.VMEM((2, PEER_TE, tm), F32),
                        pltpu.VMEM((2, PEER_TE, tm), BF16)],
        compiler_params=_cparams(1),
        name="peer_experts",
    )(x1t, u_b, vt_blocks, *gates)


def _final_kernel(x1_ref, yt_ref, p_ref, wple_ref, wpg_ref, g_ref, b_ref, x2_ref, x2b_ref):
    x1 = x1_ref[...]
    emb = jnp.dot(p_ref[...].astype(BF16), wple_ref[...], preferred_element_type=F32)
    gate = jax.nn.sigmoid(jnp.dot(x1.astype(BF16), wpg_ref[...], preferred_element_type=F32))
    z = DEEPNORM_ALPHA * x1 + yt_ref[...].T + emb * gate
    y = _layer_norm(z, g_ref[...], b_ref[...])
    x2_ref[...] = y
    x2b_ref[...] = y.astype(BF16)


def _final(x1, yt, p, wple_b, wpg_b, g, b, tm):
    t = x1.shape[0]
    pdim = p.shape[1]
    return pl.pallas_call(
        _final_kernel,
        grid=(t // tm,),
        in_specs=[pl.BlockSpec((tm, D_MODEL), lambda i: (i, 0)),
                  pl.BlockSpec((D_MODEL, tm), lambda i: (0, i)),
                  pl.BlockSpec((tm, pdim), lambda i: (i, 0)),
                  pl.BlockSpec((pdim, D_MODEL), lambda i: (0, 0)),
                  pl.BlockSpec((D_MODEL, D_MODEL), lambda i: (0, 0)),
                  pl.BlockSpec((1, D_MODEL), lambda i: (0, 0)),
                  pl.BlockSpec((1, D_MODEL), lambda i: (0, 0))],
        out_specs=[pl.BlockSpec((tm, D_MODEL), lambda i: (i, 0)),
                   pl.BlockSpec((tm, D_MODEL), lambda i: (i, 0))],
        out_shape=[jax.ShapeDtypeStruct((t, D_MODEL), F32),
                   jax.ShapeDtypeStruct((t, D_MODEL), BF16)],
        compiler_params=_cparams(1),
        name="ple_ln2",
    )(x1, yt, p, wple_b, wpg_b, g, b)


def _tiles(batch, seq):
    t = batch * seq
    return dict(
        mm_tm=min(512, t), mm_tn=1536,
        mixer_ts=min(512, seq),
        out_tm=min(256, t),
        route_tp=min(256, t),
        peer_tm=min(512, t),
        final_tm=min(256, t),
    )


def kernel(x, p, w_in, b_in, w_pool, pool_scale, dw_w, dw_b, cn_g, cn_b, rel_bias, w_out, ln1_g, ln1_b,
           w_q, sub_keys, u_tab, v_tab, w_ple, w_pg, ln2_g, ln2_b):
    batch, seq, d = x.shape
    depth = w_in.shape[0]
    t = batch * seq
    assert d == D_MODEL and depth == DEPTH and seq % ATTN_TQ == 0
    tl = _tiles(batch, seq)

    xf = x.reshape(t, d)
    xb = xf.astype(BF16)
    for i in range(depth):
        w_in_b = w_in[i].astype(BF16)
        b_row = b_in[i].reshape(1, -1)
        hpc = _matmul_bias(xb, w_in_b, b_row, F32, tl["mm_tm"], tl["mm_tn"], 0, PC_WIDTH,
                           "inproj_pool_conv")
        qkv = _matmul_bias(xb, w_in_b, b_row, BF16, tl["mm_tm"], tl["mm_tn"], PC_WIDTH,
                           3 * ATTN_WIDTH, "inproj_qkv")
        ypc = _mixer(hpc, w_pool[i].astype(BF16), pool_scale[i].reshape(1, -1),
                     dw_w[i].reshape(CONV_KERNEL, CONV_WIDTH), dw_b[i].reshape(1, -1),
                     cn_g[i].reshape(1, -1), cn_b[i].reshape(1, -1), batch, seq, tl["mixer_ts"])
        ya = _attention(qkv, _attn_bias_tile(rel_bias[i]), batch, seq)
        x1, x1t = _outproj(ypc, ya, w_out[i].astype(BF16), xf, ln1_g[i].reshape(1, -1),
                           ln1_b[i].reshape(1, -1), tl["out_tm"])
        gates = _route(x1t, w_q[i].T.astype(BF16), sub_keys[i].astype(BF16), tl["route_tp"])
        vt_blocks = v_tab[i].astype(BF16).reshape(PEER_NJ, PEER_TE, D_MODEL).transpose(0, 2, 1)
        yt = _experts(x1t, u_tab[i].astype(BF16), vt_blocks, gates, tl["peer_tm"])
        xf, xb = _final(x1, yt, p[i].reshape(t, -1), w_ple[i].astype(BF16), w_pg[i].astype(BF16),
                        ln2_g[i].reshape(1, -1), ln2_b[i].reshape(1, -1), tl["final_tm"])
    return xf.reshape(batch, seq, d)
```

```python
import functools

import jax
import jax.numpy as jnp
import numpy as np
from jax import lax
from jax.experimental import pallas as pl
from jax.experimental.pallas import tpu as pltpu

F32 = jnp.float32
BF16 = jnp.bfloat16

D_MODEL = 2048
CHUNK = 64
POOL_WINDOWS = (2, 4, 8, 16)
POOL_GROUP = 128
POOL_WIDTH = 512
CONV_WIDTH = 512
CONV_KERNEL = 31
ATTN_WIDTH = 1024
ATTN_HEADS = 8
HEAD_DIM = 128
LEFT_CHUNKS = 8
BAND = (LEFT_CHUNKS + 1) * CHUNK
REL_LEFT = 128
REL_RIGHT = CHUNK - 1
PC_WIDTH = POOL_WIDTH + 2 * CONV_WIDTH
PEER_HEADS = 8
PEER_NKEYS = 128
PEER_EXPERTS = PEER_NKEYS * PEER_NKEYS
PEER_HALF = 128
PEER_TOPK = 16
LN_EPS = 1e-5
DEPTH = 4
DEEPNORM_ALPHA = (2 * DEPTH) ** 0.25

LANES = 128
SUBLANES = 8
VMEM_LIMIT = 56 * 1024 * 1024

HALO = 32
ATTN_TQ = 256
ATTN_KBLKS = 3
NEG_BIG = -1e30
SQRT_HALF = 0.7071067811865476


def _cparams(n_axes, flags=None):
    return pltpu.CompilerParams(dimension_semantics=("arbitrary",) * n_axes,
                                vmem_limit_bytes=VMEM_LIMIT, flags=flags)


def _layer_norm(z, g, b):
    mu = jnp.mean(z, axis=-1, keepdims=True)
    zc = z - mu
    var = jnp.mean(zc * zc, axis=-1, keepdims=True)
    return zc * lax.rsqrt(var + LN_EPS) * g + b


def _mm_bias_kernel(x_ref, w_ref, b_ref, o_ref):
    acc = jnp.dot(x_ref[...], w_ref[...], preferred_element_type=F32)
    o_ref[...] = (acc + b_ref[...]).astype(o_ref.dtype)


def _matmul_bias(xb, w, b, layer, out_dtype, tm, tn, col0, n, name):
    t, k = xb.shape
    assert col0 % tn == 0 and n % tn == 0
    first = col0 // tn
    return pl.pallas_call(
        _mm_bias_kernel,
        grid=(n // tn, t // tm),
        in_specs=[pl.BlockSpec((tm, k), lambda j, i: (i, 0)),
                  pl.BlockSpec((None, k, tn), lambda j, i: (layer, 0, first + j)),
                  pl.BlockSpec((None, 1, tn), lambda j, i: (layer, 0, first + j))],
        out_specs=pl.BlockSpec((tm, tn), lambda j, i: (i, j)),
        out_shape=jax.ShapeDtypeStruct((t, n), out_dtype),
        compiler_params=_cparams(2),
        name=name,
    )(xb, w, b)


def _mixer_kernel(cur_ref, halo_ref, wpool_ref, pscale_ref, dww_ref, dwb_ref, cng_ref, cnb_ref,
                  o_ref, *, ts):
    s = pl.program_id(1)
    cur = cur_ref[...]
    halo = jnp.where(s > 0, halo_ref[...], 0.0)
    ext = jnp.concatenate([halo, cur], axis=0)

    pos = s * ts + lax.broadcasted_iota(jnp.int32, (ts, POOL_GROUP), 0)
    pool_out = []
    for g, w in enumerate(POOL_WINDOWS):
        xg = ext[:, g * POOL_GROUP:(g + 1) * POOL_GROUP]
        acc = xg
        step = 1
        while step < w:
            acc = acc + pltpu.roll(acc, step, axis=0)
            step *= 2
        count = jnp.minimum(pos + 1, w).astype(F32)
        pooled = acc[HALO:] / count - xg[HALO:]
        pool_out.append(jnp.dot(pooled.astype(BF16), wpool_ref[g], preferred_element_type=F32))
    y_pool = jnp.concatenate(pool_out, axis=1) * pscale_ref[...]

    xv = ext[:, POOL_WIDTH:POOL_WIDTH + CONV_WIDTH]
    xg = ext[:, POOL_WIDTH + CONV_WIDTH:]
    hglu = xv * jax.nn.sigmoid(xg)
    acc = jnp.zeros((ts, CONV_WIDTH), F32)
    for k in range(CONV_KERNEL):
        delay = CONV_KERNEL - 1 - k
        shifted = hglu if delay == 0 else pltpu.roll(hglu, delay, axis=0)
        acc = acc + dww_ref[k:k + 1, :] * shifted[HALO:]
    hc = _layer_norm(acc + dwb_ref[...], cng_ref[...], cnb_ref[...])
    y_conv = hc * jax.nn.sigmoid(hc)
    o_ref[...] = jnp.concatenate([y_pool, y_conv], axis=1).astype(o_ref.dtype)


def _mixer(hpc, wpool, pscale, dww, dwb, cng, cnb, batch, seq, ts):
    hpc3 = hpc.reshape(batch, seq, PC_WIDTH)
    hb = ts // HALO
    out = pl.pallas_call(
        functools.partial(_mixer_kernel, ts=ts),
        grid=(batch, seq // ts),
        in_specs=[pl.BlockSpec((None, ts, PC_WIDTH), lambda b, s: (b, s, 0)),
                  pl.BlockSpec((None, HALO, PC_WIDTH), lambda b, s: (b, jnp.maximum(s * hb - 1, 0), 0)),
                  pl.BlockSpec((4, POOL_GROUP, POOL_GROUP), lambda b, s: (0, 0, 0)),
                  pl.BlockSpec((1, POOL_WIDTH), lambda b, s: (0, 0)),
                  pl.BlockSpec((CONV_KERNEL, CONV_WIDTH), lambda b, s: (0, 0)),
                  pl.BlockSpec((1, CONV_WIDTH), lambda b, s: (0, 0)),
                  pl.BlockSpec((1, CONV_WIDTH), lambda b, s: (0, 0)),
                  pl.BlockSpec((1, CONV_WIDTH), lambda b, s: (0, 0))],
        out_specs=pl.BlockSpec((None, ts, POOL_WIDTH + CONV_WIDTH), lambda b, s: (b, s, 0)),
        out_shape=jax.ShapeDtypeStruct((batch, seq, POOL_WIDTH + CONV_WIDTH), BF16),
        compiler_params=_cparams(2),
        name="pool_conv_mixer",
    )(hpc3, hpc3, wpool, pscale, dww, dwb, cng, cnb)
    return out.reshape(batch * seq, POOL_WIDTH + CONV_WIDTH)


def _attn_kernel(q_ref, k0_ref, k1_ref, k2_ref, v0_ref, v1_ref, v2_ref, bias_ref, o_ref):
    i = pl.program_id(1)
    k_refs = (k0_ref, k1_ref, k2_ref)
    v_refs = (v0_ref, v1_ref, v2_ref)
    scale = HEAD_DIM ** -0.5
    for h in range(ATTN_HEADS):
        cols = slice(h * HEAD_DIM, (h + 1) * HEAD_DIM)
        q = q_ref[:, cols]
        scores = []
        for r in range(ATTN_KBLKS):
            sc = lax.dot_general(q, k_refs[r][:, cols], (((1,), (1,)), ((), ())),
                                 preferred_element_type=F32)
            sc = sc * scale + bias_ref[h, :, r * ATTN_TQ:(r + 1) * ATTN_TQ]
            scores.append(jnp.where(i - (ATTN_KBLKS - 1) + r >= 0, sc, NEG_BIG))
        m = scores[0].max(axis=1, keepdims=True)
        for r in range(1, ATTN_KBLKS):
            m = jnp.maximum(m, scores[r].max(axis=1, keepdims=True))
        denom = jnp.zeros_like(m)
        acc = jnp.zeros((ATTN_TQ, HEAD_DIM), F32)
        for r in range(ATTN_KBLKS):
            e = jnp.exp(scores[r] - m)
            denom = denom + e.sum(axis=1, keepdims=True)
            acc = acc + jnp.dot(e.astype(BF16), v_refs[r][:, cols], preferred_element_type=F32)
        o_ref[:, cols] = (acc / denom).astype(o_ref.dtype)


def _attention(qkv, bias_tile, batch, seq):
    qkv3 = qkv.reshape(batch, seq, 3 * ATTN_WIDTH)
    blk = (None, ATTN_TQ, ATTN_WIDTH)

    def kv_spec(col, r):
        return pl.BlockSpec(blk, lambda b, i: (b, jnp.maximum(i - (ATTN_KBLKS - 1) + r, 0), col))

    out = pl.pallas_call(
        _attn_kernel,
        grid=(batch, seq // ATTN_TQ),
        in_specs=[pl.BlockSpec(blk, lambda b, i: (b, i, 0))]
                 + [kv_spec(1, r) for r in range(ATTN_KBLKS)]
                 + [kv_spec(2, r) for r in range(ATTN_KBLKS)]
                 + [pl.BlockSpec((ATTN_HEADS, ATTN_TQ, ATTN_KBLKS * ATTN_TQ), lambda b, i: (0, 0, 0))],
        out_specs=pl.BlockSpec(blk, lambda b, i: (b, i, 0)),
        out_shape=jax.ShapeDtypeStruct((batch, seq, ATTN_WIDTH), BF16),
        compiler_params=_cparams(2),
        name="chunk_attention",
    )(qkv3, qkv3, qkv3, qkv3, qkv3, qkv3, qkv3, bias_tile)
    return out.reshape(batch * seq, ATTN_WIDTH)


def _attn_bias_tile(rel_table):
    nq, nk = ATTN_TQ, ATTN_KBLKS * ATTN_TQ
    period = nq + nk
    m = np.arange(period)
    offset = np.where(m < nk, m, m - period)
    dist = (ATTN_KBLKS - 1) * ATTN_TQ - offset
    rel_idx = np.clip(dist, -REL_RIGHT, REL_LEFT) + REL_RIGHT
    per_offset = rel_table[:, rel_idx].astype(F32)
    toe = jnp.tile(per_offset, (1, nq))[:, :nq * (period - 1)]
    toe = toe.reshape(ATTN_HEADS, nq, period - 1)[:, :, :nk]
    qpos = np.arange(nq)[:, None] + (ATTN_KBLKS - 1) * ATTN_TQ
    kpos = np.arange(nk)[None, :]
    chunk_start = (qpos // CHUNK) * CHUNK
    in_band = (kpos >= chunk_start - LEFT_CHUNKS * CHUNK) & (kpos < chunk_start + CHUNK)
    return jnp.where(in_band[None], toe, NEG_BIG)


def _outproj_kernel(ypc_ref, ya_ref, w_ref, x_ref, g_ref, b_ref, x1_ref, x1t_ref):
    half = POOL_WIDTH + CONV_WIDTH
    mixed = jnp.dot(ypc_ref[...], w_ref[:half, :], preferred_element_type=F32)
    mixed = mixed + jnp.dot(ya_ref[...], w_ref[half:, :], preferred_element_type=F32)
    y = _layer_norm(DEEPNORM_ALPHA * x_ref[...] + mixed, g_ref[...], b_ref[...])
    x1_ref[...] = y
    x1t_ref[...] = y.T.astype(BF16)


def _outproj(ypc, ya, w_out_b, layer, x, g, b, tm):
    t = x.shape[0]
    return pl.pallas_call(
        _outproj_kernel,
        grid=(t // tm,),
        in_specs=[pl.BlockSpec((tm, POOL_WIDTH + CONV_WIDTH), lambda i: (i, 0)),
                  pl.BlockSpec((tm, ATTN_WIDTH), lambda i: (i, 0)),
                  pl.BlockSpec((None, D_MODEL, D_MODEL), lambda i: (layer, 0, 0)),
                  pl.BlockSpec((tm, D_MODEL), lambda i: (i, 0)),
                  pl.BlockSpec((1, D_MODEL), lambda i: (0, 0)),
                  pl.BlockSpec((1, D_MODEL), lambda i: (0, 0))],
        out_specs=[pl.BlockSpec((tm, D_MODEL), lambda i: (i, 0)),
                   pl.BlockSpec((D_MODEL, tm), lambda i: (0, i))],
        out_shape=[jax.ShapeDtypeStruct((t, D_MODEL), F32),
                   jax.ShapeDtypeStruct((D_MODEL, t), BF16)],
        compiler_params=_cparams(1),
        name="outproj_ln1",
    )(ypc, ya, w_out_b, x, g, b)


def _col_reduce(x, op):
    parts = [x[r:r + SUBLANES] for r in range(0, x.shape[0], SUBLANES)]
    while len(parts) > 1:
        parts = [op(parts[i], parts[i + 1]) if i + 1 < len(parts) else parts[i]
                 for i in range(0, len(parts), 2)]
    last = parts[0]
    if op is jnp.maximum:
        return jnp.max(last, axis=0, keepdims=True)
    if op is jnp.minimum:
        return jnp.min(last, axis=0, keepdims=True)
    return jnp.sum(last, axis=0, keepdims=True)


def _extract_topk_exact(cur, order, k):
    rank = jnp.full(cur.shape, float(k), F32)
    vals = []
    for r in range(k):
        m = _col_reduce(cur, jnp.maximum)
        first = _col_reduce(jnp.where(cur == m, order, 1e9), jnp.minimum)
        hit = order == first
        rank = jnp.where(hit, float(r), rank)
        cur = jnp.where(hit, -jnp.inf, cur)
        vals.append(m)
    return vals, rank


def _col_maxima(a, k):
    m = _col_reduce(a, jnp.maximum)
    vals = [m]
    for _ in range(1, k):
        m = _col_reduce(jnp.where(a < m, a, -jnp.inf), jnp.maximum)
        vals.append(m)
    return vals


def _has_tie(a, kth, k):
    count = _col_reduce(jnp.where(a >= kth, 1.0, 0.0), jnp.add)
    return jnp.max(count) > float(k)


def _rows_to_ref(ref, rows):
    for r, v in enumerate(rows):
        ref[r:r + 1, :] = v


_MID_K2 = tuple(range(1, 8))


N_CAND_ROWS = 2 * SUBLANES + len(_MID_K2) * SUBLANES + SUBLANES
ROUTE_UNROLL = 2


def _route_kernel(xt_ref, wqt_ref, keys_ref, n_ref, ea_ref, rb_ref, eb_ref, qt_scr, s_scr, *, tp):
    qt_scr[...] = jnp.dot(wqt_ref[...], xt_ref[...], preferred_element_type=F32)
    key_order = lax.broadcasted_iota(jnp.int32, (PEER_NKEYS, tp), 0).astype(F32)
    row8 = lax.broadcasted_iota(jnp.int32, (SUBLANES, tp), 0).astype(F32)
    row16 = lax.broadcasted_iota(jnp.int32, (2 * SUBLANES, tp), 0).astype(F32)

    def candidates(s1, s2):
        cands = [s1 + s2[0:1, :]]
        order = [row16 * PEER_TOPK]
        for k2 in _MID_K2:
            valid = row8 < float(PEER_TOPK // (k2 + 1))
            cands.append(jnp.where(valid, s1[:SUBLANES] + s2[k2:k2 + 1, :], -jnp.inf))
            order.append(row8 * PEER_TOPK + float(k2))
        cands.append(s1[0:1, :] + s2[SUBLANES:])
        order.append(row8 + float(SUBLANES))
        return jnp.concatenate(cands, axis=0), jnp.concatenate(order, axis=0)

    def picks_per_rank(sel):
        self_ = sel.astype(F32)
        n_lo = self_[0:SUBLANES]
        for g in range(len(_MID_K2)):
            lo = 2 * SUBLANES + g * SUBLANES
            n_lo = n_lo + self_[lo:lo + SUBLANES]
        tail = jnp.sum(self_[2 * SUBLANES + len(_MID_K2) * SUBLANES:], axis=0, keepdims=True)
        n_lo = n_lo + jnp.where(row8 == 0.0, tail, 0.0)
        return jnp.concatenate([n_lo, self_[SUBLANES:2 * SUBLANES]], axis=0)

    def emit(h, sc, s1, s2, cand, sel, nkey, rank_b):
        top = s1[0:1, :] + s2[0:1, :]
        z = jnp.sum(jnp.where(sel, jnp.exp(cand - top), 0.0), axis=0, keepdims=True)
        n_ref[h] = nkey
        ea_ref[h] = jnp.exp(sc[0] - s1[0:1, :]) / z
        rb_ref[h] = rank_b.astype(BF16)
        eb_ref[h] = jnp.exp(sc[1] - s2[0:1, :]).astype(BF16)

    def scores(h):
        base = pl.multiple_of(h * 2 * PEER_HALF, 2 * PEER_HALF)
        sc = []
        for p in range(2):
            qh = qt_scr[pl.ds(base + p * PEER_HALF, PEER_HALF), :].astype(BF16)
            sc.append(jnp.dot(keys_ref[p], qh, preferred_element_type=F32))
        return sc

    def cheap_pass(h, sc, slot):
        tops = [_col_maxima(sc[p], PEER_TOPK) for p in range(2)]
        tied = jnp.logical_or(_has_tie(sc[0], tops[0][-1], PEER_TOPK),
                              _has_tie(sc[1], tops[1][-1], PEER_TOPK))
        for p in range(2):
            _rows_to_ref(s_scr.at[slot, p], tops[p])
        s1, s2 = s_scr[slot, 0], s_scr[slot, 1]
        cand, _ = candidates(s1, s2)
        kth = _col_maxima(cand, PEER_TOPK)[-1]
        sel = cand >= kth
        tied = jnp.logical_or(tied, _has_tie(cand, kth, PEER_TOPK))
        n16 = picks_per_rank(sel)
        nkey = jnp.zeros((PEER_NKEYS, tp), F32)
        rank_b = jnp.full((PEER_NKEYS, tp), float(PEER_TOPK), F32)
        for r in reversed(range(PEER_TOPK)):
            nkey = jnp.where(sc[0] == s1[r:r + 1, :], n16[r:r + 1, :], nkey)
            rank_b = jnp.where(sc[1] >= s2[r:r + 1, :], float(r), rank_b)
        emit(h, sc, s1, s2, cand, sel, nkey, rank_b)
        return tied

    def tie_aware_pass(h, sc, slot):
        ranks = []
        for p in range(2):
            vals, rank = _extract_topk_exact(sc[p], key_order, PEER_TOPK)
            _rows_to_ref(s_scr.at[slot, p], vals)
            ranks.append(rank)
        t1, t2 = s_scr[slot, 0], s_scr[slot, 1]
        tcand, order = candidates(t1, t2)
        _, crank = _extract_topk_exact(tcand, order, PEER_TOPK)
        tsel = crank < float(PEER_TOPK)
        m16 = picks_per_rank(tsel)
        mkey = jnp.zeros((PEER_NKEYS, tp), F32)
        for k1 in range(PEER_TOPK):
            mkey = jnp.where(ranks[0] == float(k1), m16[k1:k1 + 1, :], mkey)
        emit(h, sc, t1, t2, tcand, tsel, mkey, ranks[1])

    def head_group(it, carry):
        heads = [it * ROUTE_UNROLL + s for s in range(ROUTE_UNROLL)]
        scs = [scores(h) for h in heads]
        tied = [cheap_pass(h, sc, s) for s, (h, sc) in enumerate(zip(heads, scs))]
        for s, h in enumerate(heads):
            pl.when(tied[s])(functools.partial(tie_aware_pass, h, scs[s], s))
        return carry

    lax.fori_loop(0, PEER_HEADS // ROUTE_UNROLL, head_group, 0)


def _route(x1t, wqt_b, layer, keys_b, tp):
    t = x1t.shape[1]
    gate_spec = pl.BlockSpec((PEER_HEADS, PEER_NKEYS, tp), lambda i: (0, 0, i))
    gate_shapes = [jax.ShapeDtypeStruct((PEER_HEADS, PEER_NKEYS, t), dt) for dt in (F32, F32, BF16, BF16)]
    return pl.pallas_call(
        functools.partial(_route_kernel, tp=tp),
        grid=(t // tp,),
        in_specs=[pl.BlockSpec((D_MODEL, tp), lambda i: (0, i)),
                  pl.BlockSpec((None, D_MODEL, D_MODEL), lambda i: (layer, 0, 0)),
                  pl.BlockSpec((2, PEER_NKEYS, PEER_HALF), lambda i: (0, 0, 0))],
        out_specs=[gate_spec] * 4,
        out_shape=gate_shapes,
        scratch_shapes=[pltpu.VMEM((D_MODEL, tp), F32),
                        pltpu.VMEM((ROUTE_UNROLL, 2, PEER_TOPK, tp), F32)],
        compiler_params=_cparams(1),
        name="peer_route",
    )(x1t, wqt_b, keys_b)


PEER_TE = 1024
PEER_NJ = PEER_EXPERTS // PEER_TE
PEER_GROUPS = PEER_TE // PEER_NKEYS
PEER_ROWS = 16
PEER_LANES = 256
PEER_D1_ROWS = 1024
PEER_D2_ROWS = 512


def _experts_tick(p, g, xt_ref, u_ref, vt_ref, n_ref, ea_ref, rb_ref, eb_ref, y_ref, h_scr, a_scr, tm,
                  n_blocks):
    live = jnp.logical_and(g >= 1, g <= n_blocks).astype(F32)
    blk = jnp.clip(g - 1, 0, n_blocks - 1) % PEER_NJ

    def gate_chunk(ii, lt):
        i = blk * PEER_GROUPS + ii
        lanes = slice(lt * PEER_LANES, (lt + 1) * PEER_LANES)
        shape = (PEER_ROWS, PEER_LANES)
        n_rows = [jnp.broadcast_to(n_ref[h, pl.ds(i, 1), :][:, lanes].astype(BF16), shape)
                  for h in range(PEER_HEADS)]
        ea_rows = [jnp.broadcast_to((ea_ref[h, pl.ds(i, 1), :][:, lanes] * live).astype(BF16), shape)
                   for h in range(PEER_HEADS)]
        zero = jnp.zeros(shape, BF16)
        for r0 in range(0, PEER_NKEYS, PEER_ROWS):
            gate = None
            for h in range(PEER_HEADS):
                term = jnp.where(rb_ref[h, r0:r0 + PEER_ROWS, lanes] < n_rows[h],
                                 eb_ref[h, r0:r0 + PEER_ROWS, lanes], zero) * ea_rows[h]
                gate = term if gate is None else gate + term
            rows = slice(ii * PEER_NKEYS + r0, ii * PEER_NKEYS + r0 + PEER_ROWS)
            hh = h_scr[1 - p, rows, lanes]
            act = 0.5 * hh * (1.0 + lax.erf(hh * SQRT_HALF))
            a_scr[1 - p, rows, lanes] = act.astype(BF16) * gate

    chunks = [(ii, lt) for ii in range(PEER_GROUPS) for lt in range(tm // PEER_LANES)]
    d1_pieces = list(range(0, PEER_TE, PEER_D1_ROWS))
    d2_pieces = list(range(0, D_MODEL, PEER_D2_ROWS))
    n1, n2 = len(d1_pieces), len(d2_pieces)
    for s, (ii, lt) in enumerate(chunks):
        for er in d1_pieces[s * n1 // len(chunks):(s + 1) * n1 // len(chunks)]:
            h_scr[p, er:er + PEER_D1_ROWS, :] = jnp.dot(
                u_ref[er:er + PEER_D1_ROWS, :], xt_ref[...], preferred_element_type=F32)
        gate_chunk(ii, lt)
        for dr in d2_pieces[s * n2 // len(chunks):(s + 1) * n2 // len(chunks)]:
            y_ref[dr:dr + PEER_D2_ROWS, :] += jnp.dot(
                vt_ref[dr:dr + PEER_D2_ROWS, :], a_scr[p], preferred_element_type=F32)


def _experts_kernel(xt_ref, u_ref, vt_ref, n_ref, ea_ref, rb_ref, eb_ref, y_ref, h_scr, a_scr, *,
                    tm, n_blocks):
    g = pl.program_id(0)

    @pl.when(g == 0)
    def _():
        h_scr[...] = jnp.zeros_like(h_scr)
        a_scr[...] = jnp.zeros_like(a_scr)

    @pl.when(jnp.clip(g - 2, 0, n_blocks - 1) % PEER_NJ == 0)
    def _():
        y_ref[...] = jnp.zeros_like(y_ref)

    args = (xt_ref, u_ref, vt_ref, n_ref, ea_ref, rb_ref, eb_ref, y_ref, h_scr, a_scr, tm, n_blocks)

    @pl.when(g % 2 == 0)
    def _():
        _experts_tick(0, g, *args)

    @pl.when(g % 2 == 1)
    def _():
        _experts_tick(1, g, *args)


def _experts(x1t, u_b, vt_blocks, layer, gates, tm):
    t = x1t.shape[1]
    n_blocks = (t // tm) * PEER_NJ

    def stage(lag):
        return lambda g: jnp.clip(g - lag, 0, n_blocks - 1)

    gate_spec = pl.BlockSpec((PEER_HEADS, PEER_NKEYS, tm), lambda g: (0, 0, stage(1)(g) // PEER_NJ))
    return pl.pallas_call(
        functools.partial(_experts_kernel, tm=tm, n_blocks=n_blocks),
        grid=(n_blocks + 2,),
        in_specs=[pl.BlockSpec((D_MODEL, tm), lambda g: (0, stage(0)(g) // PEER_NJ)),
                  pl.BlockSpec((None, PEER_TE, D_MODEL), lambda g: (layer, stage(0)(g) % PEER_NJ, 0)),
                  pl.BlockSpec((None, None, D_MODEL, PEER_TE),
                               lambda g: (layer, stage(2)(g) % PEER_NJ, 0, 0))]
                 + [gate_spec] * 4,
        out_specs=pl.BlockSpec((D_MODEL, tm), lambda g: (0, stage(2)(g) // PEER_NJ)),
        out_shape=jax.ShapeDtypeStruct((D_MODEL, t), F32),
        scratch_shapes=[pltpu.VMEM((2, PEER_TE, tm), F32),
                        pltpu.VMEM((2, PEER_TE, tm), BF16)],
        compiler_params=_cparams(1),
        name="peer_experts",
    )(x1t, u_b, vt_blocks, *gates)


def _final_kernel(x1_ref, yt_ref, p_ref, wple_ref, wpg_ref, g_ref, b_ref, x2_ref, x2b_ref):
    x1 = x1_ref[...]
    emb = jnp.dot(p_ref[...].astype(BF16), wple_ref[...], preferred_element_type=F32)
    gate = jax.nn.sigmoid(jnp.dot(x1.astype(BF16), wpg_ref[...], preferred_element_type=F32))
    z = DEEPNORM_ALPHA * x1 + yt_ref[...].T + emb * gate
    y = _layer_norm(z, g_ref[...], b_ref[...])
    x2_ref[...] = y
    x2b_ref[...] = y.astype(BF16)


def _final(x1, yt, p, wple_b, wpg_b, layer, g, b, tm):
    t = x1.shape[0]
    pdim = p.shape[-1]
    return pl.pallas_call(
        _final_kernel,
        grid=(t // tm,),
        in_specs=[pl.BlockSpec((tm, D_MODEL), lambda i: (i, 0)),
                  pl.BlockSpec((D_MODEL, tm), lambda i: (0, i)),
                  pl.BlockSpec((None, tm, pdim), lambda i: (layer, i, 0)),
                  pl.BlockSpec((None, pdim, D_MODEL), lambda i: (layer, 0, 0)),
                  pl.BlockSpec((None, D_MODEL, D_MODEL), lambda i: (layer, 0, 0)),
                  pl.BlockSpec((1, D_MODEL), lambda i: (0, 0)),
                  pl.BlockSpec((1, D_MODEL), lambda i: (0, 0))],
        out_specs=[pl.BlockSpec((tm, D_MODEL), lambda i: (i, 0)),
                   pl.BlockSpec((tm, D_MODEL), lambda i: (i, 0))],
        out_shape=[jax.ShapeDtypeStruct((t, D_MODEL), F32),
                   jax.ShapeDtypeStruct((t, D_MODEL), BF16)],
        compiler_params=_cparams(1),
        name="ple_ln2",
    )(x1, yt, p, wple_b, wpg_b, g, b)


def _tiles(batch, seq):
    t = batch * seq
    return dict(
        mm_tm=min(512, t), mm_tn=1536,
        mixer_ts=min(512, seq),
        out_tm=min(256, t),
        route_tp=min(256, t),
        peer_tm=min(512, t),
        final_tm=min(256, t),
    )


def kernel(x, p, w_in, b_in, w_pool, pool_scale, dw_w, dw_b, cn_g, cn_b, rel_bias, w_out, ln1_g, ln1_b,
           w_q, sub_keys, u_tab, v_tab, w_ple, w_pg, ln2_g, ln2_b):
    batch, seq, d = x.shape
    depth = w_in.shape[0]
    t = batch * seq
    assert d == D_MODEL and depth == DEPTH and seq % ATTN_TQ == 0
    tl = _tiles(batch, seq)

    w_in_b = w_in.astype(BF16)
    b_rows = b_in.reshape(depth, 1, -1)
    w_out_b = w_out.astype(BF16)
    wqt_b = w_q.astype(BF16).transpose(0, 2, 1)
    u_b = u_tab.astype(BF16)
    vt_blocks = v_tab.astype(BF16).reshape(depth, PEER_NJ, PEER_TE, D_MODEL).transpose(0, 1, 3, 2)
    w_ple_b = w_ple.astype(BF16)
    w_pg_b = w_pg.astype(BF16)
    p_rows = p.reshape(depth, t, -1)

    xf = x.reshape(t, d)
    xb = xf.astype(BF16)
    for i in range(depth):
        hpc = _matmul_bias(xb, w_in_b, b_rows, i, F32, tl["mm_tm"], tl["mm_tn"], 0, PC_WIDTH,
                           "inproj_pool_conv")
        qkv = _matmul_bias(xb, w_in_b, b_rows, i, BF16, tl["mm_tm"], tl["mm_tn"], PC_WIDTH,
                           3 * ATTN_WIDTH, "inproj_qkv")
        ypc = _mixer(hpc, w_pool[i].astype(BF16), pool_scale[i].reshape(1, -1),
                     dw_w[i].reshape(CONV_KERNEL, CONV_WIDTH), dw_b[i].reshape(1, -1),
                     cn_g[i].reshape(1, -1), cn_b[i].reshape(1, -1), batch, seq, tl["mixer_ts"])
        ya = _attention(qkv, _attn_bias_tile(rel_bias[i]), batch, seq)
        x1, x1t = _outproj(ypc, ya, w_out_b, i, xf, ln1_g[i].reshape(1, -1),
                           ln1_b[i].reshape(1, -1), tl["out_tm"])
        gates = _route(x1t, wqt_b, i, sub_keys[i].astype(BF16), tl["route_tp"])
        yt = _experts(x1t, u_b, vt_blocks, i, gates, tl["peer_tm"])
        xf, xb = _final(x1, yt, p_rows, w_ple_b, w_pg_b, i, ln2_g[i].reshape(1, -1),
                        ln2_b[i].reshape(1, -1), tl["final_tm"])
    return xf.reshape(batch, seq, d)
```

```python
import functools

import jax
import jax.numpy as jnp
import numpy as np
from jax import lax
from jax.experimental import pallas as pl
from jax.experimental.pallas import tpu as pltpu

F32 = jnp.float32
BF16 = jnp.bfloat16

D_MODEL = 2048
CHUNK = 64
POOL_WINDOWS = (2, 4, 8, 16)
POOL_GROUP = 128
POOL_WIDTH = 512
CONV_WIDTH = 512
CONV_KERNEL = 31
ATTN_WIDTH = 1024
ATTN_HEADS = 8
HEAD_DIM = 128
LEFT_CHUNKS = 8
BAND = (LEFT_CHUNKS + 1) * CHUNK
REL_LEFT = 128
REL_RIGHT = CHUNK - 1
PC_WIDTH = POOL_WIDTH + 2 * CONV_WIDTH
PEER_HEADS = 8
PEER_NKEYS = 128
PEER_EXPERTS = PEER_NKEYS * PEER_NKEYS
PEER_HALF = 128
PEER_TOPK = 16
LN_EPS = 1e-5
DEPTH = 4
DEEPNORM_ALPHA = (2 * DEPTH) ** 0.25

LANES = 128
SUBLANES = 8
VMEM_LIMIT = 56 * 1024 * 1024

HALO = 32
ATTN_TQ = 256
ATTN_KBLKS = 3
NEG_BIG = -1e30
SQRT_HALF = 0.7071067811865476


def _cparams(n_axes, flags=None):
    return pltpu.CompilerParams(dimension_semantics=("arbitrary",) * n_axes,
                                vmem_limit_bytes=VMEM_LIMIT, flags=flags)


def _layer_norm(z, g, b):
    mu = jnp.mean(z, axis=-1, keepdims=True)
    zc = z - mu
    var = jnp.mean(zc * zc, axis=-1, keepdims=True)
    return zc * lax.rsqrt(var + LN_EPS) * g + b


def _mm_bias_kernel(x_ref, w_ref, b_ref, o_ref):
    acc = jnp.dot(x_ref[...], w_ref[...], preferred_element_type=F32)
    o_ref[...] = (acc + b_ref[...]).astype(o_ref.dtype)


def _matmul_bias(xb, w, b, layer, out_dtype, tm, tn, col0, n, name):
    t, k = xb.shape
    assert col0 % tn == 0 and n % tn == 0
    first = col0 // tn
    return pl.pallas_call(
        _mm_bias_kernel,
        grid=(n // tn, t // tm),
        in_specs=[pl.BlockSpec((tm, k), lambda j, i: (i, 0)),
                  pl.BlockSpec((None, k, tn), lambda j, i: (layer, 0, first + j)),
                  pl.BlockSpec((None, 1, tn), lambda j, i: (layer, 0, first + j))],
        out_specs=pl.BlockSpec((tm, tn), lambda j, i: (i, j)),
        out_shape=jax.ShapeDtypeStruct((t, n), out_dtype),
        compiler_params=_cparams(2),
        name=name,
    )(xb, w, b)


def _mixer_kernel(cur_ref, halo_ref, wpool_ref, pscale_ref, dww_ref, dwb_ref, cng_ref, cnb_ref,
                  o_ref, *, ts):
    s = pl.program_id(1)
    cur = cur_ref[...]
    halo = jnp.where(s > 0, halo_ref[...], 0.0)
    ext = jnp.concatenate([halo, cur], axis=0)

    pos = s * ts + lax.broadcasted_iota(jnp.int32, (ts, POOL_GROUP), 0)
    pool_out = []
    for g, w in enumerate(POOL_WINDOWS):
        xg = ext[:, g * POOL_GROUP:(g + 1) * POOL_GROUP]
        acc = xg
        step = 1
        while step < w:
            acc = acc + pltpu.roll(acc, step, axis=0)
            step *= 2
        count = jnp.minimum(pos + 1, w).astype(F32)
        pooled = acc[HALO:] / count - xg[HALO:]
        pool_out.append(jnp.dot(pooled.astype(BF16), wpool_ref[g], preferred_element_type=F32))
    y_pool = jnp.concatenate(pool_out, axis=1) * pscale_ref[...]

    xv = ext[:, POOL_WIDTH:POOL_WIDTH + CONV_WIDTH]
    xg = ext[:, POOL_WIDTH + CONV_WIDTH:]
    hglu = xv * jax.nn.sigmoid(xg)
    acc = jnp.zeros((ts, CONV_WIDTH), F32)
    for k in range(CONV_KERNEL):
        delay = CONV_KERNEL - 1 - k
        shifted = hglu if delay == 0 else pltpu.roll(hglu, delay, axis=0)
        acc = acc + dww_ref[k:k + 1, :] * shifted[HALO:]
    hc = _layer_norm(acc + dwb_ref[...], cng_ref[...], cnb_ref[...])
    y_conv = hc * jax.nn.sigmoid(hc)
    o_ref[...] = jnp.concatenate([y_pool, y_conv], axis=1).astype(o_ref.dtype)


def _mixer(hpc, wpool, pscale, dww, dwb, cng, cnb, batch, seq, ts):
    hpc3 = hpc.reshape(batch, seq, PC_WIDTH)
    hb = ts // HALO
    out = pl.pallas_call(
        functools.partial(_mixer_kernel, ts=ts),
        grid=(batch, seq // ts),
        in_specs=[pl.BlockSpec((None, ts, PC_WIDTH), lambda b, s: (b, s, 0)),
                  pl.BlockSpec((None, HALO, PC_WIDTH), lambda b, s: (b, jnp.maximum(s * hb - 1, 0), 0)),
                  pl.BlockSpec((4, POOL_GROUP, POOL_GROUP), lambda b, s: (0, 0, 0)),
                  pl.BlockSpec((1, POOL_WIDTH), lambda b, s: (0, 0)),
                  pl.BlockSpec((CONV_KERNEL, CONV_WIDTH), lambda b, s: (0, 0)),
                  pl.BlockSpec((1, CONV_WIDTH), lambda b, s: (0, 0)),
                  pl.BlockSpec((1, CONV_WIDTH), lambda b, s: (0, 0)),
                  pl.BlockSpec((1, CONV_WIDTH), lambda b, s: (0, 0))],
        out_specs=pl.BlockSpec((None, ts, POOL_WIDTH + CONV_WIDTH), lambda b, s: (b, s, 0)),
        out_shape=jax.ShapeDtypeStruct((batch, seq, POOL_WIDTH + CONV_WIDTH), BF16),
        compiler_params=_cparams(2),
        name="pool_conv_mixer",
    )(hpc3, hpc3, wpool, pscale, dww, dwb, cng, cnb)
    return out.reshape(batch * seq, POOL_WIDTH + CONV_WIDTH)


def _attn_kernel(q_ref, k0_ref, k1_ref, k2_ref, v0_ref, v1_ref, v2_ref, bias_ref, o_ref):
    i = pl.program_id(1)
    k_refs = (k0_ref, k1_ref, k2_ref)
    v_refs = (v0_ref, v1_ref, v2_ref)
    scale = HEAD_DIM ** -0.5
    for h in range(ATTN_HEADS):
        cols = slice(h * HEAD_DIM, (h + 1) * HEAD_DIM)
        q = q_ref[:, cols]
        scores = []
        for r in range(ATTN_KBLKS):
            sc = lax.dot_general(q, k_refs[r][:, cols], (((1,), (1,)), ((), ())),
                                 preferred_element_type=F32)
            sc = sc * scale + bias_ref[h, :, r * ATTN_TQ:(r + 1) * ATTN_TQ]
            scores.append(jnp.where(i - (ATTN_KBLKS - 1) + r >= 0, sc, NEG_BIG))
        m = scores[0].max(axis=1, keepdims=True)
        for r in range(1, ATTN_KBLKS):
            m = jnp.maximum(m, scores[r].max(axis=1, keepdims=True))
        denom = jnp.zeros_like(m)
        acc = jnp.zeros((ATTN_TQ, HEAD_DIM), F32)
        for r in range(ATTN_KBLKS):
            e = jnp.exp(scores[r] - m)
            denom = denom + e.sum(axis=1, keepdims=True)
            acc = acc + jnp.dot(e.astype(BF16), v_refs[r][:, cols], preferred_element_type=F32)
        o_ref[:, cols] = (acc / denom).astype(o_ref.dtype)


def _attention(qkv, bias_tile, batch, seq):
    qkv3 = qkv.reshape(batch, seq, 3 * ATTN_WIDTH)
    blk = (None, ATTN_TQ, ATTN_WIDTH)

    def kv_spec(col, r):
        return pl.BlockSpec(blk, lambda b, i: (b, jnp.maximum(i - (ATTN_KBLKS - 1) + r, 0), col))

    out = pl.pallas_call(
        _attn_kernel,
        grid=(batch, seq // ATTN_TQ),
        in_specs=[pl.BlockSpec(blk, lambda b, i: (b, i, 0))]
                 + [kv_spec(1, r) for r in range(ATTN_KBLKS)]
                 + [kv_spec(2, r) for r in range(ATTN_KBLKS)]
                 + [pl.BlockSpec((ATTN_HEADS, ATTN_TQ, ATTN_KBLKS * ATTN_TQ), lambda b, i: (0, 0, 0))],
        out_specs=pl.BlockSpec(blk, lambda b, i: (b, i, 0)),
        out_shape=jax.ShapeDtypeStruct((batch, seq, ATTN_WIDTH), BF16),
        compiler_params=_cparams(2),
        name="chunk_attention",
    )(qkv3, qkv3, qkv3, qkv3, qkv3, qkv3, qkv3, bias_tile)
    return out.reshape(batch * seq, ATTN_WIDTH)


def _attn_bias_tile(rel_table):
    nq, nk = ATTN_TQ, ATTN_KBLKS * ATTN_TQ
    period = nq + nk
    m = np.arange(period)
    offset = np.where(m < nk, m, m - period)
    dist = (ATTN_KBLKS - 1) * ATTN_TQ - offset
    rel_idx = np.clip(dist, -REL_RIGHT, REL_LEFT) + REL_RIGHT
    per_offset = rel_table[:, rel_idx].astype(F32)
    toe = jnp.tile(per_offset, (1, nq))[:, :nq * (period - 1)]
    toe = toe.reshape(ATTN_HEADS, nq, period - 1)[:, :, :nk]
    qpos = np.arange(nq)[:, None] + (ATTN_KBLKS - 1) * ATTN_TQ
    kpos = np.arange(nk)[None, :]
    chunk_start = (qpos // CHUNK) * CHUNK
    in_band = (kpos >= chunk_start - LEFT_CHUNKS * CHUNK) & (kpos < chunk_start + CHUNK)
    return jnp.where(in_band[None], toe, NEG_BIG)


def _outproj_kernel(ypc_ref, ya_ref, w_ref, x_ref, g_ref, b_ref, x1_ref, x1t_ref):
    half = POOL_WIDTH + CONV_WIDTH
    mixed = jnp.dot(ypc_ref[...], w_ref[:half, :], preferred_element_type=F32)
    mixed = mixed + jnp.dot(ya_ref[...], w_ref[half:, :], preferred_element_type=F32)
    y = _layer_norm(DEEPNORM_ALPHA * x_ref[...] + mixed, g_ref[...], b_ref[...])
    x1_ref[...] = y
    x1t_ref[...] = y.T.astype(BF16)


def _outproj(ypc, ya, w_out_b, layer, x, g, b, tm):
    t = x.shape[0]
    return pl.pallas_call(
        _outproj_kernel,
        grid=(t // tm,),
        in_specs=[pl.BlockSpec((tm, POOL_WIDTH + CONV_WIDTH), lambda i: (i, 0)),
                  pl.BlockSpec((tm, ATTN_WIDTH), lambda i: (i, 0)),
                  pl.BlockSpec((None, D_MODEL, D_MODEL), lambda i: (layer, 0, 0)),
                  pl.BlockSpec((tm, D_MODEL), lambda i: (i, 0)),
                  pl.BlockSpec((1, D_MODEL), lambda i: (0, 0)),
                  pl.BlockSpec((1, D_MODEL), lambda i: (0, 0))],
        out_specs=[pl.BlockSpec((tm, D_MODEL), lambda i: (i, 0)),
                   pl.BlockSpec((D_MODEL, tm), lambda i: (0, i))],
        out_shape=[jax.ShapeDtypeStruct((t, D_MODEL), F32),
                   jax.ShapeDtypeStruct((D_MODEL, t), BF16)],
        compiler_params=_cparams(1),
        name="outproj_ln1",
    )(ypc, ya, w_out_b, x, g, b)


def _col_reduce(x, op):
    parts = [x[r:r + SUBLANES] for r in range(0, x.shape[0], SUBLANES)]
    while len(parts) > 1:
        parts = [op(parts[i], parts[i + 1]) if i + 1 < len(parts) else parts[i]
                 for i in range(0, len(parts), 2)]
    last = parts[0]
    if op is jnp.maximum:
        return jnp.max(last, axis=0, keepdims=True)
    if op is jnp.minimum:
        return jnp.min(last, axis=0, keepdims=True)
    return jnp.sum(last, axis=0, keepdims=True)


def _extract_topk_exact(cur, order, k):
    rank = jnp.full(cur.shape, float(k), F32)
    vals = []
    for r in range(k):
        m = _col_reduce(cur, jnp.maximum)
        first = _col_reduce(jnp.where(cur == m, order, 1e9), jnp.minimum)
        hit = order == first
        rank = jnp.where(hit, float(r), rank)
        cur = jnp.where(hit, -jnp.inf, cur)
        vals.append(m)
    return vals, rank


def _col_maxima(a, k):
    m = _col_reduce(a, jnp.maximum)
    vals = [m]
    for _ in range(1, k):
        m = _col_reduce(jnp.where(a < m, a, -jnp.inf), jnp.maximum)
        vals.append(m)
    return vals


def _has_tie(a, kth, k):
    count = _col_reduce(jnp.where(a >= kth, 1.0, 0.0), jnp.add)
    return jnp.max(count) > float(k)


def _rows_to_ref(ref, rows):
    for r, v in enumerate(rows):
        ref[r:r + 1, :] = v


_MID_K2 = tuple(range(1, 8))


N_CAND_ROWS = 2 * SUBLANES + len(_MID_K2) * SUBLANES + SUBLANES
ROUTE_UNROLL = 2


def _route_kernel(xt_ref, wqt_ref, keys_ref, n_ref, ea_ref, rb_ref, eb_ref, qt_scr, s_scr, *, tp):
    qt_scr[...] = jnp.dot(wqt_ref[...], xt_ref[...], preferred_element_type=F32)
    key_order = lax.broadcasted_iota(jnp.int32, (PEER_NKEYS, tp), 0).astype(F32)
    row8 = lax.broadcasted_iota(jnp.int32, (SUBLANES, tp), 0).astype(F32)
    row16 = lax.broadcasted_iota(jnp.int32, (2 * SUBLANES, tp), 0).astype(F32)

    def candidates(s1, s2):
        cands = [s1 + s2[0:1, :]]
        order = [row16 * PEER_TOPK]
        for k2 in _MID_K2:
            valid = row8 < float(PEER_TOPK // (k2 + 1))
            cands.append(jnp.where(valid, s1[:SUBLANES] + s2[k2:k2 + 1, :], -jnp.inf))
            order.append(row8 * PEER_TOPK + float(k2))
        cands.append(s1[0:1, :] + s2[SUBLANES:])
        order.append(row8 + float(SUBLANES))
        return jnp.concatenate(cands, axis=0), jnp.concatenate(order, axis=0)

    def picks_per_rank(sel):
        self_ = sel.astype(F32)
        n_lo = self_[0:SUBLANES]
        for g in range(len(_MID_K2)):
            lo = 2 * SUBLANES + g * SUBLANES
            n_lo = n_lo + self_[lo:lo + SUBLANES]
        tail = jnp.sum(self_[2 * SUBLANES + len(_MID_K2) * SUBLANES:], axis=0, keepdims=True)
        n_lo = n_lo + jnp.where(row8 == 0.0, tail, 0.0)
        return jnp.concatenate([n_lo, self_[SUBLANES:2 * SUBLANES]], axis=0)

    def emit(h, sc, s1, s2, cand, sel, nkey, rank_b):
        top = s1[0:1, :] + s2[0:1, :]
        z = jnp.sum(jnp.where(sel, jnp.exp(cand - top), 0.0), axis=0, keepdims=True)
        n_ref[h] = nkey
        ea_ref[h] = jnp.exp(sc[0] - s1[0:1, :]) / z
        rb_ref[h] = rank_b.astype(BF16)
        eb_ref[h] = jnp.exp(sc[1] - s2[0:1, :]).astype(BF16)

    def scores(h):
        base = pl.multiple_of(h * 2 * PEER_HALF, 2 * PEER_HALF)
        sc = []
        for p in range(2):
            qh = qt_scr[pl.ds(base + p * PEER_HALF, PEER_HALF), :].astype(BF16)
            sc.append(jnp.dot(keys_ref[p], qh, preferred_element_type=F32))
        return sc

    def cheap_pass(h, sc, slot):
        tops = [_col_maxima(sc[p], PEER_TOPK) for p in range(2)]
        tied = jnp.logical_or(_has_tie(sc[0], tops[0][-1], PEER_TOPK),
                              _has_tie(sc[1], tops[1][-1], PEER_TOPK))
        for p in range(2):
            _rows_to_ref(s_scr.at[slot, p], tops[p])
        s1, s2 = s_scr[slot, 0], s_scr[slot, 1]
        cand, _ = candidates(s1, s2)
        kth = _col_maxima(cand, PEER_TOPK)[-1]
        sel = cand >= kth
        tied = jnp.logical_or(tied, _has_tie(cand, kth, PEER_TOPK))
        n16 = picks_per_rank(sel)
        nkey = jnp.zeros((PEER_NKEYS, tp), F32)
        rank_b = jnp.full((PEER_NKEYS, tp), float(PEER_TOPK), F32)
        for r in reversed(range(PEER_TOPK)):
            nkey = jnp.where(sc[0] == s1[r:r + 1, :], n16[r:r + 1, :], nkey)
            rank_b = jnp.where(sc[1] >= s2[r:r + 1, :], float(r), rank_b)
        emit(h, sc, s1, s2, cand, sel, nkey, rank_b)
        return tied

    def tie_aware_pass(h, sc, slot):
        ranks = []
        for p in range(2):
            vals, rank = _extract_topk_exact(sc[p], key_order, PEER_TOPK)
            _rows_to_ref(s_scr.at[slot, p], vals)
            ranks.append(rank)
        t1, t2 = s_scr[slot, 0], s_scr[slot, 1]
        tcand, order = candidates(t1, t2)
        _, crank = _extract_topk_exact(tcand, order, PEER_TOPK)
        tsel = crank < float(PEER_TOPK)
        m16 = picks_per_rank(tsel)
        mkey = jnp.zeros((PEER_NKEYS, tp), F32)
        for k1 in range(PEER_TOPK):
            mkey = jnp.where(ranks[0] == float(k1), m16[k1:k1 + 1, :], mkey)
        emit(h, sc, t1, t2, tcand, tsel, mkey, ranks[1])

    def head_group(it, carry):
        heads = [it * ROUTE_UNROLL + s for s in range(ROUTE_UNROLL)]
        scs = [scores(h) for h in heads]
        tied = [cheap_pass(h, sc, s) for s, (h, sc) in enumerate(zip(heads, scs))]
        for s, h in enumerate(heads):
            pl.when(tied[s])(functools.partial(tie_aware_pass, h, scs[s], s))
        return carry

    lax.fori_loop(0, PEER_HEADS // ROUTE_UNROLL, head_group, 0)


def _route(x1t, wqt_b, layer, keys_b, tp):
    t = x1t.shape[1]
    gate_spec = pl.BlockSpec((PEER_HEADS, PEER_NKEYS, tp), lambda i: (0, 0, i))
    gate_shapes = [jax.ShapeDtypeStruct((PEER_HEADS, PEER_NKEYS, t), dt) for dt in (F32, F32, BF16, BF16)]
    return pl.pallas_call(
        functools.partial(_route_kernel, tp=tp),
        grid=(t // tp,),
        in_specs=[pl.BlockSpec((D_MODEL, tp), lambda i: (0, i)),
                  pl.BlockSpec((None, D_MODEL, D_MODEL), lambda i: (layer, 0, 0)),
                  pl.BlockSpec((2, PEER_NKEYS, PEER_HALF), lambda i: (0, 0, 0))],
        out_specs=[gate_spec] * 4,
        out_shape=gate_shapes,
        scratch_shapes=[pltpu.VMEM((D_MODEL, tp), F32),
                        pltpu.VMEM((ROUTE_UNROLL, 2, PEER_TOPK, tp), F32)],
        compiler_params=_cparams(1),
        name="peer_route",
    )(x1t, wqt_b, keys_b)


PEER_TE = 1024
PEER_NJ = PEER_EXPERTS // PEER_TE
PEER_GROUPS = PEER_TE // PEER_NKEYS
PEER_ROWS = 16
PEER_LANES = 256
PEER_D1_ROWS = 1024
PEER_D2_ROWS = 512


def _experts_tick(p, g, xt_ref, u_ref, vt_ref, n_ref, ea_ref, rb_ref, eb_ref, y_ref, h_scr, a_scr, tm,
                  n_blocks):
    live = jnp.logical_and(g >= 1, g <= n_blocks).astype(F32)
    blk = jnp.clip(g - 1, 0, n_blocks - 1) % PEER_NJ

    first_keys = pl.ds(pl.multiple_of(blk * PEER_GROUPS, PEER_GROUPS), PEER_GROUPS)

    def gate_chunk(ii, lt):
        lanes = slice(lt * PEER_LANES, (lt + 1) * PEER_LANES)
        shape = (PEER_ROWS, PEER_LANES)
        n_rows = [jnp.broadcast_to(n_ref[h, first_keys, lanes][ii:ii + 1], shape).astype(BF16)
                  for h in range(PEER_HEADS)]
        ea_rows = [jnp.broadcast_to(ea_ref[h, first_keys, lanes][ii:ii + 1] * live, shape).astype(BF16)
                   for h in range(PEER_HEADS)]
        zero = jnp.zeros(shape, BF16)
        for r0 in range(0, PEER_NKEYS, PEER_ROWS):
            gate = None
            for h in range(PEER_HEADS):
                term = jnp.where(rb_ref[h, r0:r0 + PEER_ROWS, lanes] < n_rows[h],
                                 eb_ref[h, r0:r0 + PEER_ROWS, lanes], zero) * ea_rows[h]
                gate = term if gate is None else gate + term
            rows = slice(ii * PEER_NKEYS + r0, ii * PEER_NKEYS + r0 + PEER_ROWS)
            hh = h_scr[1 - p, rows, lanes]
            act = 0.5 * hh * (1.0 + lax.erf(hh * SQRT_HALF))
            a_scr[1 - p, rows, lanes] = act.astype(BF16) * gate

    chunks = [(ii, lt) for ii in range(PEER_GROUPS) for lt in range(tm // PEER_LANES)]
    d1_pieces = list(range(0, PEER_TE, PEER_D1_ROWS))
    d2_pieces = list(range(0, D_MODEL, PEER_D2_ROWS))
    n1, n2 = len(d1_pieces), len(d2_pieces)
    for s, (ii, lt) in enumerate(chunks):
        for er in d1_pieces[s * n1 // len(chunks):(s + 1) * n1 // len(chunks)]:
            h_scr[p, er:er + PEER_D1_ROWS, :] = jnp.dot(
                u_ref[er:er + PEER_D1_ROWS, :], xt_ref[...], preferred_element_type=F32)
        gate_chunk(ii, lt)
        for dr in d2_pieces[s * n2 // len(chunks):(s + 1) * n2 // len(chunks)]:
            y_ref[dr:dr + PEER_D2_ROWS, :] += jnp.dot(
                vt_ref[dr:dr + PEER_D2_ROWS, :], a_scr[p], preferred_element_type=F32)


def _experts_kernel(xt_ref, u_ref, vt_ref, n_ref, ea_ref, rb_ref, eb_ref, y_ref, h_scr, a_scr, *,
                    tm, n_blocks):
    g = pl.program_id(0)

    @pl.when(g == 0)
    def _():
        h_scr[...] = jnp.zeros_like(h_scr)
        a_scr[...] = jnp.zeros_like(a_scr)

    @pl.when(jnp.clip(g - 2, 0, n_blocks - 1) % PEER_NJ == 0)
    def _():
        y_ref[...] = jnp.zeros_like(y_ref)

    args = (xt_ref, u_ref, vt_ref, n_ref, ea_ref, rb_ref, eb_ref, y_ref, h_scr, a_scr, tm, n_blocks)

    @pl.when(g % 2 == 0)
    def _():
        _experts_tick(0, g, *args)

    @pl.when(g % 2 == 1)
    def _():
        _experts_tick(1, g, *args)


def _experts(x1t, u_b, vt_blocks, layer, gates, tm):
    t = x1t.shape[1]
    n_blocks = (t // tm) * PEER_NJ

    def stage(lag):
        return lambda g: jnp.clip(g - lag, 0, n_blocks - 1)

    gate_spec = pl.BlockSpec((PEER_HEADS, PEER_NKEYS, tm), lambda g: (0, 0, stage(1)(g) // PEER_NJ))
    return pl.pallas_call(
        functools.partial(_experts_kernel, tm=tm, n_blocks=n_blocks),
        grid=(n_blocks + 2,),
        in_specs=[pl.BlockSpec((D_MODEL, tm), lambda g: (0, stage(0)(g) // PEER_NJ)),
                  pl.BlockSpec((None, PEER_TE, D_MODEL), lambda g: (layer, stage(0)(g) % PEER_NJ, 0)),
                  pl.BlockSpec((None, None, D_MODEL, PEER_TE),
                               lambda g: (layer, stage(2)(g) % PEER_NJ, 0, 0))]
                 + [gate_spec] * 4,
        out_specs=pl.BlockSpec((D_MODEL, tm), lambda g: (0, stage(2)(g) // PEER_NJ)),
        out_shape=jax.ShapeDtypeStruct((D_MODEL, t), F32),
        scratch_shapes=[pltpu.VMEM((2, PEER_TE, tm), F32),
                        pltpu.VMEM((2, PEER_TE, tm), BF16)],
        compiler_params=_cparams(1),
        name="peer_experts",
    )(x1t, u_b, vt_blocks, *gates)


def _final_kernel(x1_ref, yt_ref, p_ref, wple_ref, wpg_ref, g_ref, b_ref, x2_ref, x2b_ref):
    x1 = x1_ref[...]
    emb = jnp.dot(p_ref[...].astype(BF16), wple_ref[...], preferred_element_type=F32)
    gate = jax.nn.sigmoid(jnp.dot(x1.astype(BF16), wpg_ref[...], preferred_element_type=F32))
    z = DEEPNORM_ALPHA * x1 + yt_ref[...].T + emb * gate
    y = _layer_norm(z, g_ref[...], b_ref[...])
    x2_ref[...] = y
    x2b_ref[...] = y.astype(BF16)


def _final(x1, yt, p, wple_b, wpg_b, layer, g, b, tm):
    t = x1.shape[0]
    pdim = p.shape[-1]
    return pl.pallas_call(
        _final_kernel,
        grid=(t // tm,),
        in_specs=[pl.BlockSpec((tm, D_MODEL), lambda i: (i, 0)),
                  pl.BlockSpec((D_MODEL, tm), lambda i: (0, i)),
                  pl.BlockSpec((None, tm, pdim), lambda i: (layer, i, 0)),
                  pl.BlockSpec((None, pdim, D_MODEL), lambda i: (layer, 0, 0)),
                  pl.BlockSpec((None, D_MODEL, D_MODEL), lambda i: (layer, 0, 0)),
                  pl.BlockSpec((1, D_MODEL), lambda i: (0, 0)),
                  pl.BlockSpec((1, D_MODEL), lambda i: (0, 0))],
        out_specs=[pl.BlockSpec((tm, D_MODEL), lambda i: (i, 0)),
                   pl.BlockSpec((tm, D_MODEL), lambda i: (i, 0))],
        out_shape=[jax.ShapeDtypeStruct((t, D_MODEL), F32),
                   jax.ShapeDtypeStruct((t, D_MODEL), BF16)],
        compiler_params=_cparams(1),
        name="ple_ln2",
    )(x1, yt, p, wple_b, wpg_b, g, b)


def _tiles(batch, seq):
    t = batch * seq
    return dict(
        mm_tm=min(512, t), mm_tn=1536,
        mixer_ts=min(512, seq),
        out_tm=min(512, t),
        route_tp=min(256, t),
        peer_tm=min(512, t),
        final_tm=min(512, t),
    )


def kernel(x, p, w_in, b_in, w_pool, pool_scale, dw_w, dw_b, cn_g, cn_b, rel_bias, w_out, ln1_g, ln1_b,
           w_q, sub_keys, u_tab, v_tab, w_ple, w_pg, ln2_g, ln2_b):
    batch, seq, d = x.shape
    depth = w_in.shape[0]
    t = batch * seq
    assert d == D_MODEL and depth == DEPTH and seq % ATTN_TQ == 0
    tl = _tiles(batch, seq)

    w_in_b = w_in.astype(BF16)
    b_rows = b_in.reshape(depth, 1, -1)
    w_out_b = w_out.astype(BF16)
    wqt_b = w_q.astype(BF16).transpose(0, 2, 1)
    u_b = u_tab.astype(BF16)
    vt_blocks = v_tab.astype(BF16).reshape(depth, PEER_NJ, PEER_TE, D_MODEL).transpose(0, 1, 3, 2)
    w_ple_b = w_ple.astype(BF16)
    w_pg_b = w_pg.astype(BF16)
    p_rows = p.reshape(depth, t, -1)

    xf = x.reshape(t, d)
    xb = xf.astype(BF16)
    for i in range(depth):
        hpc = _matmul_bias(xb, w_in_b, b_rows, i, F32, tl["mm_tm"], tl["mm_tn"], 0, PC_WIDTH,
                           "inproj_pool_conv")
        qkv = _matmul_bias(xb, w_in_b, b_rows, i, BF16, tl["mm_tm"], tl["mm_tn"], PC_WIDTH,
                           3 * ATTN_WIDTH, "inproj_qkv")
        ypc = _mixer(hpc, w_pool[i].astype(BF16), pool_scale[i].reshape(1, -1),
                     dw_w[i].reshape(CONV_KERNEL, CONV_WIDTH), dw_b[i].reshape(1, -1),
                     cn_g[i].reshape(1, -1), cn_b[i].reshape(1, -1), batch, seq, tl["mixer_ts"])
        ya = _attention(qkv, _attn_bias_tile(rel_bias[i]), batch, seq)
        x1, x1t = _outproj(ypc, ya, w_out_b, i, xf, ln1_g[i].reshape(1, -1),
                           ln1_b[i].reshape(1, -1), tl["out_tm"])
        gates = _route(x1t, wqt_b, i, sub_keys[i].astype(BF16), tl["route_tp"])
        yt = _experts(x1t, u_b, vt_blocks, i, gates, tl["peer_tm"])
        xf, xb = _final(x1, yt, p_rows, w_ple_b, w_pg_b, i, ln2_g[i].reshape(1, -1),
                        ln2_b[i].reshape(1, -1), tl["final_tm"])
    return xf.reshape(batch, seq, d)
```
